```python
import jax, jax.numpy as jnp
from jax import lax
import numpy as np

D_MODEL = 1024
BATCH = 4
SEQ = 8192
DEPTH = 2

EPS = 1e-6
LN_EPS = 1e-5
CONV_WIDTH = 512
CONV_K = 31
N_HEADS = 8
QK_NOPE_DIM = 64
QK_ROPE_DIM = 32
V_HEAD_DIM = 64
Q_LORA_RANK = 256
KV_LORA_RANK = 128
ROPE_BASE = 10000.0
Q_BLOCK = 128
ATTN_WIDTH = N_HEADS * V_HEAD_DIM
IN_EVEN = 2 * CONV_WIDTH + Q_LORA_RANK + KV_LORA_RANK + QK_ROPE_DIM
SPLIT_EVEN = [CONV_WIDTH, 2 * CONV_WIDTH, 2 * CONV_WIDTH + Q_LORA_RANK,
              2 * CONV_WIDTH + Q_LORA_RANK + KV_LORA_RANK]
MIX_EVEN = CONV_WIDTH + ATTN_WIDTH
SSM_WIDTH = 512
SSM_GROUP = 16
SSM_GROUPS = SSM_WIDTH // SSM_GROUP
SSM_STATE = 64
DT_MIN = 0.001
DT_MAX = 0.1
D_FF = 2816
FFN_K = 3

kernel_name = "hybrid_conv_mla_s5_convffn"


def rms_norm(x, g):
    xf = x.astype(jnp.float32)
    y = xf * lax.rsqrt(jnp.mean(xf * xf, axis=-1, keepdims=True) + EPS)
    return (y * g.astype(jnp.float32)).astype(x.dtype)


def layer_norm(x, g, b):
    xf = x.astype(jnp.float32)
    mu = jnp.mean(xf, axis=-1, keepdims=True)
    var = jnp.mean(jnp.square(xf - mu), axis=-1, keepdims=True)
    y = (xf - mu) * lax.rsqrt(var + LN_EPS)
    return (y * g.astype(jnp.float32) + b.astype(jnp.float32)).astype(x.dtype)


def causal_dwconv(u, w, b):
    k, c = w.shape
    y = lax.conv_general_dilated(u, w[:, None, :].astype(u.dtype), window_strides=(1,),
                                 padding=[(k - 1, 0)],
                                 dimension_numbers=('NWC', 'WIO', 'NWC'),
                                 feature_group_count=c)
    return y + b.astype(u.dtype)


def rope(t, pos):
    half = QK_ROPE_DIM // 2
    inv = ROPE_BASE ** (-jnp.arange(half, dtype=jnp.float32) / half)
    ang = pos.astype(jnp.float32)[:, None] * inv[None, :]
    cos, sin = jnp.cos(ang)[:, None, :], jnp.sin(ang)[:, None, :]
    tf = t.astype(jnp.float32)
    t1, t2 = tf[..., :half], tf[..., half:]
    return jnp.concatenate([t1 * cos - t2 * sin, t1 * sin + t2 * cos], axis=-1).astype(t.dtype)


def mla_attention(q_nope, q_rope, k_nope, k_rope, v):
    b, s, h, _ = q_nope.shape
    nb = s // Q_BLOCK
    scale = (QK_NOPE_DIM + QK_ROPE_DIM) ** -0.5
    k_pos = jnp.arange(s)

    def blocks(t):
        return jnp.moveaxis(t.reshape((b, nb, Q_BLOCK) + t.shape[2:]), 1, 0)

    def one_block(args):
        qn, qr, i = args
        scores = (jnp.einsum('bqhd,bkhd->bhqk', qn, k_nope).astype(jnp.float32)
                  + jnp.einsum('bqhr,bkr->bhqk', qr, k_rope).astype(jnp.float32)) * scale
        q_pos = i * Q_BLOCK + jnp.arange(Q_BLOCK)
        mask = k_pos[None, :] <= q_pos[:, None]
        scores = jnp.where(mask[None, None], scores, -jnp.inf)
        p = jax.nn.softmax(scores, axis=-1).astype(v.dtype)
        return jnp.einsum('bhqk,bkhd->bqhd', p, v)

    out = lax.map(one_block, (blocks(q_nope), blocks(q_rope), jnp.arange(nb)))
    return jnp.moveaxis(out, 0, 1).reshape(b, s, h * V_HEAD_DIM)


def conv_attn_mixer(xn, w_in, conv_w, conv_b, conv_ln_g, conv_ln_b,
                    q_norm, kv_norm, w_uq, w_ukv, w_out):
    b, s, _ = xn.shape
    h = xn @ w_in
    glu_a, glu_g, c_q, c_kv, k_r = jnp.split(h, SPLIT_EVEN, axis=-1)
    u = glu_a * jax.nn.sigmoid(glu_g)
    u = causal_dwconv(u, conv_w, conv_b)
    u = jax.nn.silu(layer_norm(u, conv_ln_g, conv_ln_b))
    pos = jnp.arange(s)
    q = (rms_norm(c_q, q_norm) @ w_uq).reshape(b, s, N_HEADS, QK_NOPE_DIM + QK_ROPE_DIM)
    q_nope, q_rope = q[..., :QK_NOPE_DIM], rope(q[..., QK_NOPE_DIM:], pos)
    kv = (rms_norm(c_kv, kv_norm) @ w_ukv).reshape(b, s, N_HEADS, QK_NOPE_DIM + V_HEAD_DIM)
    k_nope, v = kv[..., :QK_NOPE_DIM], kv[..., QK_NOPE_DIM:]
    k_rope = rope(k_r[:, :, None, :], pos)[:, :, 0, :]
    attn = mla_attention(q_nope, q_rope, k_nope, k_rope, v)
    return jnp.concatenate([u, attn], axis=-1) @ w_out


def s5_mixer(xn, w_in, log_dt, a_re, a_im, b_re, b_im, c_re, c_im, d_skip, w_glu, b_glu):
    f32 = jnp.float32
    bsz, s, _ = xn.shape
    u = (xn @ w_in).astype(f32).reshape(bsz, s, SSM_GROUPS, SSM_GROUP)
    dt = jnp.exp(log_dt.astype(f32))[:, None]
    ar, ai = a_re.astype(f32), a_im.astype(f32)
    mag = jnp.exp(ar * dt)
    lb_re, lb_im = mag * jnp.cos(ai * dt), mag * jnp.sin(ai * dt)
    den = ar * ar + ai * ai
    nr, ni = lb_re - 1.0, lb_im
    f_re = (nr * ar + ni * ai) / den
    f_im = (ni * ar - nr * ai) / den
    br, bi = b_re.astype(f32), b_im.astype(f32)
    bb_re = f_re[..., None] * br - f_im[..., None] * bi
    bb_im = f_re[..., None] * bi + f_im[..., None] * br
    bu_re = jnp.einsum('bsgc,gpc->bsgp', u, bb_re)
    bu_im = jnp.einsum('bsgc,gpc->bsgp', u, bb_im)
    lam_re = jnp.broadcast_to(lb_re, bu_re.shape)
    lam_im = jnp.broadcast_to(lb_im, bu_re.shape)

    def combine(e1, e2):
        a1r, a1i, b1r, b1i = e1
        a2r, a2i, b2r, b2i = e2
        return (a2r * a1r - a2i * a1i, a2r * a1i + a2i * a1r,
                a2r * b1r - a2i * b1i + b2r, a2r * b1i + a2i * b1r + b2i)

    _, _, x_re, x_im = lax.associative_scan(combine, (lam_re, lam_im, bu_re, bu_im), axis=1)
    y = (jnp.einsum('gcp,bsgp->bsgc', c_re.astype(f32), x_re)
         - jnp.einsum('gcp,bsgp->bsgc', c_im.astype(f32), x_im)
         + d_skip.astype(f32).reshape(SSM_GROUPS, SSM_GROUP) * u)
    y = jax.nn.gelu(y.reshape(bsz, s, SSM_WIDTH)).astype(xn.dtype)
    z = y @ w_glu + b_glu
    return z[..., :D_MODEL] * jax.nn.sigmoid(z[..., D_MODEL:])


def conv_ffn(xn, w_up, conv_w, conv_b, w_down):
    h = causal_dwconv(xn @ w_up, conv_w, conv_b)
    return (jax.nn.silu(h[..., :D_FF]) * h[..., D_FF:]) @ w_down


def setup_inputs(seed: int = 0) -> dict:
    key = jax.random.key(seed)
    keys = iter(jax.random.split(key, 64))
    f32 = jnp.float32

    def nrm(shape, scale):
        return jax.random.normal(next(keys), shape, f32) * scale

    def gain(n):
        return 1.0 + nrm((n,), 0.01)

    d = D_MODEL
    inp = {}
    inp["x"] = nrm((BATCH, SEQ, d), 1.0)
    inp["l0_mix_norm"] = gain(d)
    inp["l0_w_in"] = nrm((d, IN_EVEN), d ** -0.5)
    inp["l0_conv_w"] = nrm((CONV_K, CONV_WIDTH), CONV_K ** -0.5)
    inp["l0_conv_b"] = nrm((CONV_WIDTH,), 0.01)
    inp["l0_conv_ln_g"] = gain(CONV_WIDTH)
    inp["l0_conv_ln_b"] = nrm((CONV_WIDTH,), 0.01)
    inp["l0_q_norm"] = gain(Q_LORA_RANK)
    inp["l0_kv_norm"] = gain(KV_LORA_RANK)
    inp["l0_w_uq"] = nrm((Q_LORA_RANK, N_HEADS * (QK_NOPE_DIM + QK_ROPE_DIM)), Q_LORA_RANK ** -0.5)
    inp["l0_w_ukv"] = nrm((KV_LORA_RANK, N_HEADS * (QK_NOPE_DIM + V_HEAD_DIM)), KV_LORA_RANK ** -0.5)
    inp["l0_w_out"] = nrm((MIX_EVEN, d), MIX_EVEN ** -0.5)
    inp["l0_ffn_norm"] = gain(d)
    inp["l0_w_up"] = nrm((d, 2 * D_FF), d ** -0.5)
    inp["l0_ffn_conv_w"] = nrm((FFN_K, 2 * D_FF), FFN_K ** -0.5)
    inp["l0_ffn_conv_b"] = nrm((2 * D_FF,), 0.01)
    inp["l0_w_down"] = nrm((D_FF, d), D_FF ** -0.5)
    inp["l1_mix_norm"] = gain(d)
    inp["l1_w_in"] = nrm((d, SSM_WIDTH), d ** -0.5)
    inp["l1_log_dt"] = jax.random.uniform(next(keys), (SSM_GROUPS,), f32,
                                          float(np.log(DT_MIN)), float(np.log(DT_MAX)))
    inp["l1_a_re"] = -0.5 + nrm((SSM_GROUPS, SSM_STATE), 0.01)
    inp["l1_a_im"] = (jnp.pi * jnp.arange(SSM_STATE, dtype=f32))[None, :] + nrm((SSM_GROUPS, SSM_STATE), 0.01)
    inp["l1_b_re"] = nrm((SSM_GROUPS, SSM_STATE, SSM_GROUP), (2 * SSM_GROUP) ** -0.5)
    inp["l1_b_im"] = nrm((SSM_GROUPS, SSM_STATE, SSM_GROUP), (2 * SSM_GROUP) ** -0.5)
    inp["l1_c_re"] = nrm((SSM_GROUPS, SSM_GROUP, SSM_STATE), SSM_STATE ** -0.5)
    inp["l1_c_im"] = nrm((SSM_GROUPS, SSM_GROUP, SSM_STATE), SSM_STATE ** -0.5)
    inp["l1_d"] = nrm((SSM_WIDTH,), 1.0)
    inp["l1_w_glu"] = nrm((SSM_WIDTH, 2 * d), SSM_WIDTH ** -0.5)
    inp["l1_b_glu"] = nrm((2 * d,), 0.01)
    inp["l1_ffn_norm"] = gain(d)
    inp["l1_w_up"] = nrm((d, 2 * D_FF), d ** -0.5)
    inp["l1_ffn_conv_w"] = nrm((FFN_K, 2 * D_FF), FFN_K ** -0.5)
    inp["l1_ffn_conv_b"] = nrm((2 * D_FF,), 0.01)
    inp["l1_w_down"] = nrm((D_FF, d), D_FF ** -0.5)
    inp["final_norm"] = gain(d)
    return inp


def reference(x,
              l0_mix_norm, l0_w_in, l0_conv_w, l0_conv_b, l0_conv_ln_g, l0_conv_ln_b,
              l0_q_norm, l0_kv_norm, l0_w_uq, l0_w_ukv, l0_w_out,
              l0_ffn_norm, l0_w_up, l0_ffn_conv_w, l0_ffn_conv_b, l0_w_down,
              l1_mix_norm, l1_w_in, l1_log_dt, l1_a_re, l1_a_im, l1_b_re, l1_b_im,
              l1_c_re, l1_c_im, l1_d, l1_w_glu, l1_b_glu,
              l1_ffn_norm, l1_w_up, l1_ffn_conv_w, l1_ffn_conv_b, l1_w_down,
              final_norm):
    layers = [
        (conv_attn_mixer,
         (l0_w_in, l0_conv_w, l0_conv_b, l0_conv_ln_g, l0_conv_ln_b,
          l0_q_norm, l0_kv_norm, l0_w_uq, l0_w_ukv, l0_w_out),
         l0_mix_norm, (l0_w_up, l0_ffn_conv_w, l0_ffn_conv_b, l0_w_down), l0_ffn_norm),
        (s5_mixer,
         (l1_w_in, l1_log_dt, l1_a_re, l1_a_im, l1_b_re, l1_b_im,
          l1_c_re, l1_c_im, l1_d, l1_w_glu, l1_b_glu),
         l1_mix_norm, (l1_w_up, l1_ffn_conv_w, l1_ffn_conv_b, l1_w_down), l1_ffn_norm),
    ]
    for i in range(DEPTH):
        mixer, mix_params, mix_g, ffn_params, ffn_g = layers[i]
        x = x + mixer(rms_norm(x, mix_g), *mix_params)
        x = x + conv_ffn(rms_norm(x, ffn_g), *ffn_params)
    return rms_norm(x, final_norm)
```

```python
import functools
import math

import jax
import jax.numpy as jnp
from jax import lax
from jax.experimental import pallas as pl
from jax.experimental.pallas import tpu as pltpu

F32 = jnp.float32
BF16 = jnp.bfloat16

D_MODEL = 1024
EPS = 1e-6
LN_EPS = 1e-5
CONV_WIDTH = 512
CONV_K = 31
N_HEADS = 8
QK_NOPE_DIM = 64
QK_ROPE_DIM = 32
ROPE_HALF = QK_ROPE_DIM // 2
V_HEAD_DIM = 64
Q_LORA_RANK = 256
KV_LORA_RANK = 128
ROPE_BASE = 10000.0
ATTN_WIDTH = N_HEADS * V_HEAD_DIM
SSM_WIDTH = 512
SSM_GROUP = 16
SSM_GROUPS = SSM_WIDTH // SSM_GROUP
SSM_STATE = 64
D_FF = 2816
FFN_K = 3

LANES = 128
SUBLANES = 8
HEAD_PAD = LANES
QK_WIDTH = N_HEADS * HEAD_PAD
CONV_HALO = 32
FFN_HALO = SUBLANES
FFN_CHUNK = 256
FFN_NCHUNK = D_FF // FFN_CHUNK
SSM_L = 8
SSM_PAIRS = SSM_GROUPS // 2
SSM_PAIR_CH = 2 * SSM_GROUP
SSM_TILE = SSM_L * SSM_PAIR_CH
SSM_ROW = SSM_L * SSM_WIDTH
VMEM_LIMIT = 56 * 1024 * 1024


def _cparams(sem):
    return pltpu.CompilerParams(dimension_semantics=sem, vmem_limit_bytes=VMEM_LIMIT)


def _rms(x, g):
    return x * lax.rsqrt(jnp.mean(x * x, axis=-1, keepdims=True) + EPS) * g


def _sigmoid(x):
    return 1.0 / (1.0 + jnp.exp(-x))


def _rope128(t, cos, sin_up, sin_dn):
    return (t * cos + pltpu.roll(t, ROPE_HALF, axis=1) * sin_up
            + pltpu.roll(t, LANES - ROPE_HALF, axis=1) * sin_dn)


def _pre0_kernel(x_ref, g_ref, w_in_ref, rope_ref, cw_ref, cb_ref, lng_ref, lnb_ref,
                 qn_ref, kvn_ref, wuq_ref, wukv_ref,
                 u_ref, q_ref, k_ref, v_ref, ubuf, *, tm):
    sb = pl.program_id(1)
    x = x_ref[0]
    xn = _rms(x, g_ref[...]).astype(BF16)
    h = jnp.dot(xn, w_in_ref[...], preferred_element_type=F32)

    u = h[:, :CONV_WIDTH] * _sigmoid(h[:, CONV_WIDTH:2 * CONV_WIDTH])

    @pl.when(sb == 0)
    def _():
        ubuf[0:CONV_HALO, :] = jnp.zeros((CONV_HALO, CONV_WIDTH), F32)

    ubuf[CONV_HALO:CONV_HALO + tm, :] = u
    acc = jnp.broadcast_to(cb_ref[...], (tm, CONV_WIDTH))
    for k in range(CONV_K):
        off = CONV_HALO - (CONV_K - 1) + k
        acc = acc + cw_ref[k:k + 1, :] * ubuf[off:off + tm, :]
    ubuf[0:CONV_HALO, :] = ubuf[tm:tm + CONV_HALO, :]
    mu = jnp.mean(acc, axis=-1, keepdims=True)
    cen = acc - mu
    var = jnp.mean(cen * cen, axis=-1, keepdims=True)
    y = cen * lax.rsqrt(var + LN_EPS) * lng_ref[...] + lnb_ref[...]
    u_ref[0] = (y * _sigmoid(y)).astype(BF16)

    rope = rope_ref[...]
    cos, sin_up, sin_dn = rope[:, :LANES], rope[:, LANES:2 * LANES], rope[:, 2 * LANES:]
    o = 2 * CONV_WIDTH
    cq = _rms(h[:, o:o + Q_LORA_RANK], qn_ref[...]).astype(BF16)
    q = jnp.dot(cq, wuq_ref[...], preferred_element_type=F32)
    o += Q_LORA_RANK
    ckv = _rms(h[:, o:o + KV_LORA_RANK], kvn_ref[...]).astype(BF16)
    kv = jnp.dot(ckv, wukv_ref[...], preferred_element_type=F32)
    o += KV_LORA_RANK
    kr = _rope128(h[:, o:o + LANES], cos, sin_up, sin_dn)
    for hd in range(N_HEADS):
        sl = slice(hd * HEAD_PAD, (hd + 1) * HEAD_PAD)
        q_ref[0, :, sl] = _rope128(q[:, sl], cos, sin_up, sin_dn).astype(BF16)
        k_ref[0, :, sl] = (kv[:, sl] + kr).astype(BF16)
    v_ref[0] = kv[:, QK_WIDTH:].astype(BF16)


def _pre0(x, g, w_in, rope, cw, cb, lng, lnb, qn, kvn, wuq, wukv, *, tm):
    b, s, d = x.shape
    const = lambda shape: pl.BlockSpec(shape, lambda i, j: (0,) * len(shape))
    row = lambda w: pl.BlockSpec((1, tm, w), lambda i, j: (i, j, 0))
    return pl.pallas_call(
        functools.partial(_pre0_kernel, tm=tm),
        grid=(b, s // tm),
        in_specs=[row(d), const(g.shape), const(w_in.shape),
                  pl.BlockSpec((tm, 3 * LANES), lambda i, j: (j, 0)),
                  const(cw.shape), const(cb.shape), const(lng.shape), const(lnb.shape),
                  const(qn.shape), const(kvn.shape), const(wuq.shape), const(wukv.shape)],
        out_specs=[row(CONV_WIDTH), row(QK_WIDTH), row(QK_WIDTH), row(ATTN_WIDTH)],
        out_shape=[jax.ShapeDtypeStruct((b, s, CONV_WIDTH), BF16),
                   jax.ShapeDtypeStruct((b, s, QK_WIDTH), BF16),
                   jax.ShapeDtypeStruct((b, s, QK_WIDTH), BF16),
                   jax.ShapeDtypeStruct((b, s, ATTN_WIDTH), BF16)],
        scratch_shapes=[pltpu.VMEM((tm + CONV_HALO, CONV_WIDTH), F32)],
        compiler_params=_cparams(("arbitrary", "arbitrary")),
        name="pre0",
    )(x, g, w_in, rope, cw, cb, lng, lnb, qn, kvn, wuq, wukv)


NEG_BIG = -1e30


def _attn_kernel(q_ref, k_ref, v_ref, o_ref, *, tq, tk):
    qi = pl.program_id(2)
    nfull = (qi * tq) // tk
    ndiag = tq // tk
    outs = []
    for hh in range(2):
        lanes = slice(hh * HEAD_PAD, (hh + 1) * HEAD_PAD)
        q = q_ref[0, :, lanes]

        def step(j, carry, masked):
            m, l, acc = carry
            start = pl.multiple_of(j * tk, tk)
            k = k_ref[0, pl.ds(start, tk), lanes]
            v = v_ref[0, pl.ds(start, tk), :]
            s = lax.dot_general(q, k, (((1,), (1,)), ((), ())), preferred_element_type=F32)
            if masked:
                row = qi * tq + lax.broadcasted_iota(jnp.int32, (tq, tk), 0)
                col = j * tk + lax.broadcasted_iota(jnp.int32, (tq, tk), 1)
                s = jnp.where(col <= row, s, NEG_BIG)
            m_new = jnp.maximum(m, jnp.max(s, axis=-1, keepdims=True))
            alpha = jnp.exp(m - m_new)
            p = jnp.exp(s - m_new)
            l = alpha * l + jnp.sum(p, axis=-1, keepdims=True)
            acc = alpha * acc + jnp.dot(p.astype(BF16), v, preferred_element_type=F32)
            return m_new, l, acc

        carry = (jnp.full((tq, 1), NEG_BIG, F32), jnp.zeros((tq, 1), F32),
                 jnp.zeros((tq, 2 * V_HEAD_DIM), F32))
        carry = lax.fori_loop(0, nfull, functools.partial(step, masked=False), carry)
        for d in range(ndiag):
            carry = step(nfull + d, carry, True)
        _, l, acc = carry
        outs.append(acc / l)
    lane = lax.broadcasted_iota(jnp.int32, (tq, 2 * V_HEAD_DIM), 1)
    o_ref[0] = jnp.where(lane < V_HEAD_DIM, outs[0], outs[1]).astype(o_ref.dtype)


def _attention(q, k, v, *, tq, tk):
    b, s, _ = q.shape
    return pl.pallas_call(
        functools.partial(_attn_kernel, tq=tq, tk=tk),
        grid=(b, N_HEADS // 2, s // tq),
        in_specs=[pl.BlockSpec((1, tq, 2 * HEAD_PAD), lambda i, h, j: (i, j, h)),
                  pl.BlockSpec((1, s, 2 * HEAD_PAD), lambda i, h, j: (i, 0, h)),
                  pl.BlockSpec((1, s, 2 * V_HEAD_DIM), lambda i, h, j: (i, 0, h))],
        out_specs=pl.BlockSpec((1, tq, 2 * V_HEAD_DIM), lambda i, h, j: (i, j, h)),
        out_shape=jax.ShapeDtypeStruct((b, s, ATTN_WIDTH), BF16),
        compiler_params=_cparams(("arbitrary", "arbitrary", "arbitrary")),
        name="attn",
    )(q, k, v)


def _post0_kernel(x_ref, u_ref, a_ref, wu_ref, wa_ref, o_ref):
    o_ref[...] = (x_ref[...]
                  + jnp.dot(u_ref[...], wu_ref[...], preferred_element_type=F32)
                  + jnp.dot(a_ref[...], wa_ref[...], preferred_element_type=F32))


def _post0(x, u, a, wu, wa, *, tm):
    n, d = x.shape
    row = lambda w: pl.BlockSpec((tm, w), lambda i: (i, 0))
    const = lambda shape: pl.BlockSpec(shape, lambda i: (0, 0))
    return pl.pallas_call(
        _post0_kernel,
        grid=(n // tm,),
        in_specs=[row(d), row(CONV_WIDTH), row(ATTN_WIDTH), const(wu.shape), const(wa.shape)],
        out_specs=row(d),
        out_shape=jax.ShapeDtypeStruct((n, d), F32),
        compiler_params=_cparams(("arbitrary",)),
        name="post0",
    )(x, u, a, wu, wa)


def _ffn_kernel(x_ref, g_ref, wa_ref, wg_ref, cwa_ref, cwg_ref, cba_ref, cbg_ref, wd_ref, fg_ref,
                o_ref, xn_scr, hbuf_a, hbuf_g, carry_a, carry_g, *, tm, final_norm):
    sb = pl.program_id(1)
    c = pl.program_id(2)

    @pl.when(c == 0)
    def _():
        x = x_ref[0]
        xn_scr[...] = _rms(x, g_ref[...]).astype(BF16)
        o_ref[0] = x

    xn = xn_scr[...]

    def conv(w_ref, cw_ref, cb_ref, hbuf, carry):
        h = jnp.dot(xn, w_ref[...], preferred_element_type=F32)
        hbuf[0:FFN_HALO, :] = jnp.where(sb == 0, 0.0, carry[c])
        hbuf[FFN_HALO:FFN_HALO + tm, :] = h
        carry[c] = h[tm - FFN_HALO:, :]
        out = cw_ref[FFN_K - 1:FFN_K, :] * h + cb_ref[...]
        for k in range(FFN_K - 1):
            off = FFN_HALO - (FFN_K - 1) + k
            out = out + cw_ref[k:k + 1, :] * hbuf[off:off + tm, :]
        return out

    ca = conv(wa_ref, cwa_ref, cba_ref, hbuf_a, carry_a)
    cg = conv(wg_ref, cwg_ref, cbg_ref, hbuf_g, carry_g)
    act = (ca * _sigmoid(ca) * cg).astype(BF16)
    o_ref[0] += jnp.dot(act, wd_ref[...], preferred_element_type=F32)

    if final_norm:
        @pl.when(c == FFN_NCHUNK - 1)
        def _():
            o_ref[0] = _rms(o_ref[0], fg_ref[...])


def _ffn(x, g, w_up, cw, cb, w_down, fg, *, tm, final_norm):
    b, s, d = x.shape
    nc = FFN_NCHUNK
    const = lambda shape: pl.BlockSpec(shape, lambda i, j, c: (0,) * len(shape))
    return pl.pallas_call(
        functools.partial(_ffn_kernel, tm=tm, final_norm=final_norm),
        grid=(b, s // tm, nc),
        in_specs=[pl.BlockSpec((1, tm, d), lambda i, j, c: (i, j, 0)),
                  const(g.shape),
                  pl.BlockSpec((d, FFN_CHUNK), lambda i, j, c: (0, c)),
                  pl.BlockSpec((d, FFN_CHUNK), lambda i, j, c: (0, c + nc)),
                  pl.BlockSpec((SUBLANES, FFN_CHUNK), lambda i, j, c: (0, c)),
                  pl.BlockSpec((SUBLANES, FFN_CHUNK), lambda i, j, c: (0, c + nc)),
                  pl.BlockSpec((1, FFN_CHUNK), lambda i, j, c: (0, c)),
                  pl.BlockSpec((1, FFN_CHUNK), lambda i, j, c: (0, c + nc)),
                  pl.BlockSpec((FFN_CHUNK, d), lambda i, j, c: (c, 0)),
                  const(fg.shape)],
        out_specs=pl.BlockSpec((1, tm, d), lambda i, j, c: (i, j, 0)),
        out_shape=jax.ShapeDtypeStruct((b, s, d), F32),
        scratch_shapes=[pltpu.VMEM((tm, d), BF16),
                        pltpu.VMEM((tm + FFN_HALO, FFN_CHUNK), F32),
                        pltpu.VMEM((tm + FFN_HALO, FFN_CHUNK), F32),
                        pltpu.VMEM((nc, FFN_HALO, FFN_CHUNK), F32),
                        pltpu.VMEM((nc, FFN_HALO, FFN_CHUNK), F32)],
        compiler_params=_cparams(("arbitrary", "arbitrary", "arbitrary")),
        name="ffn_final" if final_norm else "ffn",
    )(x, g, w_up, w_up, cw, cw, cb, cb, w_down, fg)


def _pre1_kernel(x_ref, g_ref, w_ref, u_ref):
    xn = _rms(x_ref[...], g_ref[...]).astype(BF16)
    u_ref[...] = jnp.dot(xn, w_ref[...], preferred_element_type=F32)


def _pre1(x, g, w, *, tm):
    n, d = x.shape
    return pl.pallas_call(
        _pre1_kernel,
        grid=(n // tm,),
        in_specs=[pl.BlockSpec((tm, d), lambda i: (i, 0)),
                  pl.BlockSpec(g.shape, lambda i: (0, 0)),
                  pl.BlockSpec(w.shape, lambda i: (0, 0))],
        out_specs=pl.BlockSpec((tm, SSM_WIDTH), lambda i: (i, 0)),
        out_shape=jax.ShapeDtypeStruct((n, SSM_WIDTH), F32),
        compiler_params=_cparams(("arbitrary",)),
        name="pre1",
    )(x, g, w)


def _regroup(slabs, n_out, pieces):
    rows = slabs[0].shape[0]
    quarter = lax.broadcasted_iota(jnp.int32, (rows, LANES), 1) // SSM_PAIR_CH
    out = []
    for d in range(n_out):
        acc = None
        for dst_q, (src, src_q) in enumerate(pieces(d)):
            shift = ((dst_q - src_q) * SSM_PAIR_CH) % LANES
            piece = pltpu.roll(slabs[src], shift, axis=1) if shift else slabs[src]
            acc = piece if acc is None else jnp.where(quarter == dst_q, piece, acc)
        out.append(acc)
    return out


def _shift_rows(x, n, row):
    return jnp.where(row >= n, pltpu.roll(x, n, axis=0), 0.0)


def _ssm_kernel(u_ref, mb_ref, t_ref, mc_ref, lp_ref, d_ref, y_ref, carry, *, rows, nlev):
    sb = pl.program_id(1)

    @pl.when(sb == 0)
    def _():
        carry[...] = jnp.zeros(carry.shape, F32)

    nslab = SSM_ROW // LANES
    u_slabs = [u_ref[0, :, i * LANES:(i + 1) * LANES] for i in range(nslab)]
    a_slabs = _regroup(u_slabs, nslab,
                       lambda d: [(4 * (4 * (d % 2) + i) + (d // 2) // 4, (d // 2) % 4) for i in range(4)])
    row = lax.broadcasted_iota(jnp.int32, (rows, LANES), 0)
    first = row == 0
    y_slabs = []
    for p in range(SSM_PAIRS):
        a = jnp.concatenate(a_slabs[2 * p:2 * p + 2], axis=1)
        a16 = a.astype(BF16)
        lo = slice(p * SSM_TILE, p * SSM_TILE + LANES)
        hi = slice(p * SSM_TILE + LANES, (p + 1) * SSM_TILE)
        sloc = jnp.dot(a16, mb_ref[p], preferred_element_type=F32)
        sr, si = sloc[:, :LANES], sloc[:, LANES:]
        cr, ci = carry[0:1, lo], carry[0:1, hi]
        lr, li = lp_ref[0:1, lo], lp_ref[0:1, hi]
        sr = sr + jnp.where(first, lr * cr - li * ci, 0.0)
        si = si + jnp.where(first, lr * ci + li * cr, 0.0)
        for lev in range(nlev):
            n = 1 << lev
            lr, li = lp_ref[lev:lev + 1, lo], lp_ref[lev:lev + 1, hi]
            pr, pi = _shift_rows(sr, n, row), _shift_rows(si, n, row)
            sr, si = sr + (lr * pr - li * pi), si + (lr * pi + li * pr)
        xr = jnp.where(first, cr, pltpu.roll(sr, 1, axis=0))
        xi = jnp.where(first, ci, pltpu.roll(si, 1, axis=0))
        carry[0:1, lo] = sr[rows - 1:rows, :]
        carry[0:1, hi] = si[rows - 1:rows, :]
        xp = jnp.concatenate([xr, xi], axis=1).astype(BF16)
        y = (jnp.dot(a16, t_ref[p], preferred_element_type=F32)
             + jnp.dot(xp, mc_ref[p], preferred_element_type=F32)
             + d_ref[:, p * SSM_TILE:(p + 1) * SSM_TILE] * a)
        y = jax.nn.gelu(y, approximate=True)
        y_slabs += [y[:, :LANES], y[:, LANES:]]
    o_slabs = _regroup(y_slabs, nslab,
                       lambda s: [(2 * (4 * (s % 4) + j) + (s // 4) // 4, (s // 4) % 4) for j in range(4)])
    for i in range(nslab):
        y_ref[0, :, i * LANES:(i + 1) * LANES] = o_slabs[i].astype(y_ref.dtype)


def _ssm(u8, mb, tt, mc, lp, dd, *, rows):
    b, r, w = u8.shape
    nlev = int(math.log2(rows))
    assert 1 << nlev == rows and lp.shape[0] >= nlev
    const = lambda shape: pl.BlockSpec(shape, lambda i, j: (0,) * len(shape))
    return pl.pallas_call(
        functools.partial(_ssm_kernel, rows=rows, nlev=nlev),
        grid=(b, r // rows),
        in_specs=[pl.BlockSpec((1, rows, w), lambda i, j: (i, j, 0)),
                  const(mb.shape), const(tt.shape), const(mc.shape), const(lp.shape), const(dd.shape)],
        out_specs=pl.BlockSpec((1, rows, w), lambda i, j: (i, j, 0)),
        out_shape=jax.ShapeDtypeStruct((b, r, w), BF16),
        scratch_shapes=[pltpu.VMEM((SUBLANES, w), F32)],
        compiler_params=_cparams(("arbitrary", "arbitrary")),
        name="ssm",
    )(u8, mb, tt, mc, lp, dd)


def _ssm_tables(log_dt, a_re, a_im, b_re, b_im, c_re, c_im, d_skip, nlev):
    g, p, c, L = SSM_GROUPS, SSM_STATE, SSM_GROUP, SSM_L
    dt = jnp.exp(log_dt.astype(F32))[:, None]
    ar, ai = a_re.astype(F32), a_im.astype(F32)
    mag = jnp.exp(ar * dt)
    lb_re, lb_im = mag * jnp.cos(ai * dt), mag * jnp.sin(ai * dt)
    den = ar * ar + ai * ai
    nr, ni = lb_re - 1.0, lb_im
    f_re = (nr * ar + ni * ai) / den
    f_im = (ni * ar - nr * ai) / den
    br, bi = b_re.astype(F32), b_im.astype(F32)
    bb_re = f_re[..., None] * br - f_im[..., None] * bi
    bb_im = f_re[..., None] * bi + f_im[..., None] * br

    def lam_pow(n):
        n = jnp.asarray(n, F32)[:, None, None]
        m = jnp.exp(n * (ar * dt))
        return m * jnp.cos(n * (ai * dt)), m * jnp.sin(n * (ai * dt))

    cr, ci = c_re.astype(F32), c_im.astype(F32)
    pr, pi = lam_pow(jnp.arange(L + 1))
    lbr = pr[..., None] * bb_re[None] - pi[..., None] * bb_im[None]
    lbi = pr[..., None] * bb_im[None] + pi[..., None] * bb_re[None]
    kk = (jnp.einsum('gop,ngpc->ngoc', cr, lbr[:L]) - jnp.einsum('gop,ngpc->ngoc', ci, lbi[:L]))
    tau = jnp.arange(L)[None, :] - jnp.arange(L)[:, None]
    kt = jnp.where((tau >= 0)[:, :, None, None, None], kk[jnp.clip(tau, 0, L - 1)], 0.0)
    kt = kt.reshape(L, L, SSM_PAIRS, 2, c, c)
    eye2 = jnp.eye(2, dtype=F32)
    tt = jnp.einsum('tsqgoc,gh->qtgcsho', kt, eye2).reshape(SSM_PAIRS, SSM_TILE, SSM_TILE)
    rev = L - 1 - jnp.arange(L)
    mbr = lbr[rev].reshape(L, SSM_PAIRS, 2, p, c)
    mbi = lbi[rev].reshape(L, SSM_PAIRS, 2, p, c)
    mb = jnp.stack([mbr, mbi], axis=0)
    mb = jnp.einsum('rtqgpc,gh->qtgcrhp', mb, eye2).reshape(SSM_PAIRS, SSM_TILE, 4 * p)
    p1r, p1i = pr[1:], pi[1:]
    clr = cr[None] * p1r[:, :, None, :] - ci[None] * p1i[:, :, None, :]
    cli = cr[None] * p1i[:, :, None, :] + ci[None] * p1r[:, :, None, :]
    mcs = jnp.stack([clr, -cli], axis=0).reshape(2, L, SSM_PAIRS, 2, c, p)
    mc = jnp.einsum('rsqgop,gh->qrgpsho', mcs, eye2).reshape(SSM_PAIRS, 4 * p, SSM_TILE)
    sr, si = lam_pow(L * (2 ** jnp.arange(nlev)))
    lp = jnp.stack([sr.reshape(nlev, SSM_PAIRS, 2 * p), si.reshape(nlev, SSM_PAIRS, 2 * p)], axis=2)
    lp = lp.reshape(nlev, SSM_ROW)
    lp = jnp.pad(lp, ((0, (-nlev) % SUBLANES), (0, 0)))
    dd = jnp.broadcast_to(d_skip.astype(F32).reshape(SSM_PAIRS, 1, SSM_PAIR_CH),
                          (SSM_PAIRS, L, SSM_PAIR_CH)).reshape(1, SSM_ROW)
    return mb.astype(BF16), tt.astype(BF16), mc.astype(BF16), lp, dd


def _post1_kernel(x_ref, y_ref, w_ref, b_ref, o_ref):
    z = jnp.dot(y_ref[...], w_ref[...], preferred_element_type=F32) + b_ref[...]
    o_ref[...] = x_ref[...] + z[:, :D_MODEL] * _sigmoid(z[:, D_MODEL:])


def _post1(x, y, w, b, *, tm):
    n, d = x.shape
    return pl.pallas_call(
        _post1_kernel,
        grid=(n // tm,),
        in_specs=[pl.BlockSpec((tm, d), lambda i: (i, 0)),
                  pl.BlockSpec((tm, SSM_WIDTH), lambda i: (i, 0)),
                  pl.BlockSpec(w.shape, lambda i: (0, 0)),
                  pl.BlockSpec(b.shape, lambda i: (0, 0))],
        out_specs=pl.BlockSpec((tm, d), lambda i: (i, 0)),
        out_shape=jax.ShapeDtypeStruct((n, d), F32),
        compiler_params=_cparams(("arbitrary",)),
        name="post1",
    )(x, y, w, b)


def _head_pad_cols(nope, rope):
    k = nope.shape[0]
    parts = [nope]
    if rope is not None:
        parts.append(rope)
    used = sum(t.shape[-1] for t in parts)
    parts.append(jnp.zeros((k, N_HEADS, HEAD_PAD - used), nope.dtype))
    return jnp.concatenate(parts, axis=-1).reshape(k, QK_WIDTH)


def _rope_tables(s):
    inv = ROPE_BASE ** (-jnp.arange(ROPE_HALF, dtype=F32) / ROPE_HALF)
    ang = jnp.arange(s, dtype=F32)[:, None] * inv[None, :]
    cos, sin = jnp.cos(ang), jnp.sin(ang)
    one = jnp.ones((s, QK_NOPE_DIM), F32)
    z16 = jnp.zeros((s, ROPE_HALF), F32)
    z64 = jnp.zeros((s, QK_NOPE_DIM), F32)
    tail = jnp.zeros((s, HEAD_PAD - QK_NOPE_DIM - QK_ROPE_DIM), F32)
    cos_t = jnp.concatenate([one, cos, cos, tail], axis=1)
    sin_up = jnp.concatenate([z64, z16, sin, tail], axis=1)
    sin_dn = jnp.concatenate([z64, -sin, z16, tail], axis=1)
    return jnp.concatenate([cos_t, sin_up, sin_dn], axis=1)


def _pick(total, want):
    t = min(total, want)
    assert total % t == 0
    return t


def kernel(x, l0_mix_norm, l0_w_in, l0_conv_w, l0_conv_b, l0_conv_ln_g, l0_conv_ln_b, l0_q_norm, l0_kv_norm, l0_w_uq, l0_w_ukv, l0_w_out, l0_ffn_norm, l0_w_up, l0_ffn_conv_w, l0_ffn_conv_b, l0_w_down, l1_mix_norm, l1_w_in, l1_log_dt, l1_a_re, l1_a_im, l1_b_re, l1_b_im, l1_c_re, l1_c_im, l1_d, l1_w_glu, l1_b_glu, l1_ffn_norm, l1_w_up, l1_ffn_conv_w, l1_ffn_conv_b, l1_w_down, final_norm):
    b, s, d = x.shape
    n = b * s
    row2 = lambda v: v.reshape(1, -1).astype(F32)

    o = 2 * CONV_WIDTH + Q_LORA_RANK + KV_LORA_RANK
    w_kr = l0_w_in[:, o:]
    kr_blk = jnp.concatenate([jnp.zeros((d, QK_NOPE_DIM), F32), w_kr,
                              jnp.zeros((d, HEAD_PAD - QK_NOPE_DIM - QK_ROPE_DIM), F32)], axis=1)
    w_in0 = jnp.concatenate([l0_w_in[:, :o], kr_blk], axis=1).astype(BF16)
    scale = (QK_NOPE_DIM + QK_ROPE_DIM) ** -0.5
    wq = (l0_w_uq * scale).reshape(Q_LORA_RANK, N_HEADS, QK_NOPE_DIM + QK_ROPE_DIM)
    wuq = _head_pad_cols(wq[..., :QK_NOPE_DIM], wq[..., QK_NOPE_DIM:]).astype(BF16)
    wkv = l0_w_ukv.reshape(KV_LORA_RANK, N_HEADS, QK_NOPE_DIM + V_HEAD_DIM)
    wukv = jnp.concatenate([_head_pad_cols(wkv[..., :QK_NOPE_DIM], None),
                            wkv[..., QK_NOPE_DIM:].reshape(KV_LORA_RANK, ATTN_WIDTH)], axis=1).astype(BF16)
    cw0 = jnp.pad(l0_conv_w.astype(F32), ((0, CONV_HALO - CONV_K), (0, 0)))
    rope = _rope_tables(s)

    u0, q, k, v = _pre0(x, row2(l0_mix_norm), w_in0, rope, cw0, row2(l0_conv_b),
                        row2(l0_conv_ln_g), row2(l0_conv_ln_b), row2(l0_q_norm), row2(l0_kv_norm),
                        wuq, wukv, tm=_pick(s, 512))
    attn = _attention(q, k, v, tq=_pick(s, 512), tk=_pick(s, 512))
    w_out = l0_w_out.astype(BF16)
    x1 = _post0(x.reshape(n, d), u0.reshape(n, CONV_WIDTH), attn.reshape(n, ATTN_WIDTH),
                w_out[:CONV_WIDTH], w_out[CONV_WIDTH:], tm=_pick(n, 1024)).reshape(b, s, d)

    def ffn(xx, g, w_up, cw, cb, w_down, final):
        cwp = jnp.pad(cw.astype(F32), ((0, SUBLANES - FFN_K), (0, 0)))
        return _ffn(xx, row2(g), w_up.astype(BF16), cwp, row2(cb), w_down.astype(BF16),
                    row2(final_norm), tm=_pick(s, 1024), final_norm=final)

    x2 = ffn(x1, l0_ffn_norm, l0_w_up, l0_ffn_conv_w, l0_ffn_conv_b, l0_w_down, False)

    rows = _pick(s // SSM_L, 128)
    nlev = int(math.log2(rows))
    mb, tt, mc, lp, dd = _ssm_tables(l1_log_dt, l1_a_re, l1_a_im, l1_b_re, l1_b_im,
                                     l1_c_re, l1_c_im, l1_d, nlev)
    u1 = _pre1(x2.reshape(n, d), row2(l1_mix_norm), l1_w_in.astype(BF16), tm=_pick(n, 1024))
    y8 = _ssm(u1.reshape(b, s // SSM_L, SSM_ROW), mb, tt, mc, lp, dd, rows=rows)
    x3 = _post1(x2.reshape(n, d), y8.reshape(n, SSM_WIDTH), l1_w_glu.astype(BF16), row2(l1_b_glu),
                tm=_pick(n, 1024)).reshape(b, s, d)
    return ffn(x3, l1_ffn_norm, l1_w_up, l1_ffn_conv_w, l1_ffn_conv_b, l1_w_down, True)
```

```python
import functools
import math

import jax
import jax.numpy as jnp
from jax import lax
from jax.experimental import pallas as pl
from jax.experimental.pallas import tpu as pltpu

F32 = jnp.float32
BF16 = jnp.bfloat16

D_MODEL = 1024
EPS = 1e-6
LN_EPS = 1e-5
CONV_WIDTH = 512
CONV_K = 31
N_HEADS = 8
QK_NOPE_DIM = 64
QK_ROPE_DIM = 32
ROPE_HALF = QK_ROPE_DIM // 2
V_HEAD_DIM = 64
Q_LORA_RANK = 256
KV_LORA_RANK = 128
ROPE_BASE = 10000.0
ATTN_WIDTH = N_HEADS * V_HEAD_DIM
SSM_WIDTH = 512
SSM_GROUP = 16
SSM_GROUPS = SSM_WIDTH // SSM_GROUP
SSM_STATE = 64
D_FF = 2816
FFN_K = 3

LANES = 128
SUBLANES = 8
HEAD_PAD = LANES
QK_WIDTH = N_HEADS * HEAD_PAD
CONV_HALO = 32
CONV_ROWS = 64
FFN_HALO = SUBLANES
FFN_CHUNK = 256
FFN_NCHUNK = D_FF // FFN_CHUNK
SSM_L = 8
SSM_PAIRS = SSM_GROUPS // 2
SSM_PAIR_CH = 2 * SSM_GROUP
SSM_TILE = SSM_L * SSM_PAIR_CH
SSM_ROW = SSM_L * SSM_WIDTH
VMEM_LIMIT = 56 * 1024 * 1024


def _cparams(sem):
    return pltpu.CompilerParams(dimension_semantics=sem, vmem_limit_bytes=VMEM_LIMIT)


def _rms(x, g):
    return x * lax.rsqrt(jnp.mean(x * x, axis=-1, keepdims=True) + EPS) * g


def _sigmoid(x):
    return 1.0 / (1.0 + jnp.exp(-x))


def _rope128(t, cos, sin_up, sin_dn):
    return (t * cos + pltpu.roll(t, ROPE_HALF, axis=1) * sin_up
            + pltpu.roll(t, LANES - ROPE_HALF, axis=1) * sin_dn)


def _rope_unpack(packed):
    lane = lax.broadcasted_iota(jnp.int32, packed.shape, 1)
    r1, r2 = QK_NOPE_DIM, QK_NOPE_DIM + ROPE_HALF
    cos = jnp.where(lane < r1, 1.0, packed)
    up = pltpu.roll(packed, r2, axis=1)
    dn = pltpu.roll(packed, r1 - ROPE_HALF, axis=1)
    sin_up = jnp.where(lane >= r2, jnp.where(lane < r2 + ROPE_HALF, up, 0.0), 0.0)
    sin_dn = jnp.where(lane >= r1, jnp.where(lane < r2, dn, 0.0), 0.0)
    return cos, sin_up, sin_dn


def _pre0_kernel(x_ref, g_ref, w_in_ref, rope_ref, cw_ref, cb_ref, lng_ref, lnb_ref,
                 qn_ref, kvn_ref, wuq_ref, wukv_ref,
                 u_ref, q_ref, k_ref, vt_ref, ubuf, ushift, *, tm):
    sb = pl.program_id(1)
    x = x_ref[0]
    xn = _rms(x, g_ref[...]).astype(BF16)
    h = jnp.dot(xn, w_in_ref[...], preferred_element_type=F32)

    u = h[:, :CONV_WIDTH] * _sigmoid(h[:, CONV_WIDTH:2 * CONV_WIDTH])

    @pl.when(sb == 0)
    def _():
        ubuf[0:CONV_HALO, :] = jnp.zeros((CONV_HALO, CONV_WIDTH), F32)

    ubuf[CONV_HALO:CONV_HALO + tm, :] = u
    base = CONV_HALO - (CONV_K - 1)
    span = tm + CONV_HALO - SUBLANES
    for r in range(1, SUBLANES):
        ushift[r - 1, 0:span, :] = ubuf[r:r + span, :]
    for r0 in range(0, tm, CONV_ROWS):
        acc = jnp.broadcast_to(cb_ref[...], (CONV_ROWS, CONV_WIDTH))
        for o in range(base, base + CONV_K):
            r, al = o % SUBLANES, o - o % SUBLANES + r0
            tap = ushift[r - 1, al:al + CONV_ROWS, :] if r else ubuf[al:al + CONV_ROWS, :]
            acc = acc + cw_ref[o - base:o - base + 1, :] * tap
        mu = jnp.mean(acc, axis=-1, keepdims=True)
        cen = acc - mu
        var = jnp.mean(cen * cen, axis=-1, keepdims=True)
        y = cen * lax.rsqrt(var + LN_EPS) * lng_ref[...] + lnb_ref[...]
        u_ref[0, r0:r0 + CONV_ROWS, :] = (y * _sigmoid(y)).astype(BF16)
    ubuf[0:CONV_HALO, :] = ubuf[tm:tm + CONV_HALO, :]

    cos, sin_up, sin_dn = _rope_unpack(rope_ref[...])
    o = 2 * CONV_WIDTH
    cq = _rms(h[:, o:o + Q_LORA_RANK], qn_ref[...]).astype(BF16)
    q = jnp.dot(cq, wuq_ref[...], preferred_element_type=F32)
    o += Q_LORA_RANK
    ckv = _rms(h[:, o:o + KV_LORA_RANK], kvn_ref[...]).astype(BF16)
    kv = jnp.dot(ckv, wukv_ref[...], preferred_element_type=F32)
    o += KV_LORA_RANK
    kr = _rope128(h[:, o:o + LANES], cos, sin_up, sin_dn)
    for hd in range(N_HEADS):
        sl = slice(hd * HEAD_PAD, (hd + 1) * HEAD_PAD)
        q_ref[0, :, sl] = _rope128(q[:, sl], cos, sin_up, sin_dn).astype(BF16)
        k_ref[0, :, sl] = (kv[:, sl] + kr).astype(BF16)
    vt_ref[0, 0] = kv[:, QK_WIDTH:].T.astype(BF16)


def _pre0(x, g, w_in, rope, cw, cb, lng, lnb, qn, kvn, wuq, wukv, *, tm):
    b, s, d = x.shape
    const = lambda shape: pl.BlockSpec(shape, lambda i, j: (0,) * len(shape))
    row = lambda w: pl.BlockSpec((1, tm, w), lambda i, j: (i, j, 0))
    return pl.pallas_call(
        functools.partial(_pre0_kernel, tm=tm),
        grid=(b, s // tm),
        in_specs=[row(d), const(g.shape), const(w_in.shape),
                  pl.BlockSpec((tm, LANES), lambda i, j: (j, 0)),
                  const(cw.shape), const(cb.shape), const(lng.shape), const(lnb.shape),
                  const(qn.shape), const(kvn.shape), const(wuq.shape), const(wukv.shape)],
        out_specs=[row(CONV_WIDTH), row(QK_WIDTH), row(QK_WIDTH),
                   pl.BlockSpec((1, 1, ATTN_WIDTH, tm), lambda i, j: (i, j, 0, 0))],
        out_shape=[jax.ShapeDtypeStruct((b, s, CONV_WIDTH), BF16),
                   jax.ShapeDtypeStruct((b, s, QK_WIDTH), BF16),
                   jax.ShapeDtypeStruct((b, s, QK_WIDTH), BF16),
                   jax.ShapeDtypeStruct((b, s // tm, ATTN_WIDTH, tm), BF16)],
        scratch_shapes=[pltpu.VMEM((tm + CONV_HALO, CONV_WIDTH), F32),
                        pltpu.VMEM((SUBLANES - 1, tm + CONV_HALO, CONV_WIDTH), F32)],
        compiler_params=_cparams(("arbitrary", "arbitrary")),
        name="pre0",
    )(x, g, w_in, rope, cw, cb, lng, lnb, qn, kvn, wuq, wukv)


NEG_BIG = -1e30


def _attn_kernel(q_ref, k_ref, vt_ref, o_ref, s0_ref, s1_ref, *, t):
    qi = pl.program_id(2)
    s_refs = (s0_ref, s1_ref)

    def scores(j, hh):
        lanes = slice(hh * HEAD_PAD, (hh + 1) * HEAD_PAD)
        k = k_ref[0, pl.ds(pl.multiple_of(j * t, t), t), lanes]
        s_refs[hh][...] = lax.dot_general(k, q_ref[0, :, lanes], (((1,), (1,)), ((), ())),
                                          preferred_element_type=F32)

    def update(j, hh, state, masked):
        m, l, acc = state
        st = s_refs[hh][...]
        if masked:
            key = lax.broadcasted_iota(jnp.int32, (t, t), 0)
            qry = lax.broadcasted_iota(jnp.int32, (t, t), 1)
            st = jnp.where(key <= qry, st, NEG_BIG)
        m_new = jnp.maximum(m, jnp.max(st, axis=0, keepdims=True))
        alpha = jnp.exp2(m - m_new)
        p = jnp.exp2(st - m_new)
        l = alpha * l + jnp.sum(p, axis=0, keepdims=True)
        vt = vt_ref[0, j, hh * V_HEAD_DIM:(hh + 1) * V_HEAD_DIM, :]
        acc = alpha * acc + jnp.dot(vt, p.astype(BF16), preferred_element_type=F32)
        return m_new, l, acc

    def body(j, carry):
        scores(j, 1)
        st0 = update(j, 0, carry[:3], False)
        scores(j + 1, 0)
        st1 = update(j, 1, carry[3:], False)
        return st0 + st1

    init = (jnp.full((1, t), NEG_BIG, F32), jnp.zeros((1, t), F32), jnp.zeros((V_HEAD_DIM, t), F32)) * 2
    scores(0, 0)
    carry = lax.fori_loop(0, qi, body, init)
    scores(qi, 1)
    _, l0, acc0 = update(qi, 0, carry[:3], True)
    _, l1, acc1 = update(qi, 1, carry[3:], True)
    ot = jnp.concatenate([acc0 / l0, acc1 / l1], axis=0)
    o_ref[0] = ot.T.astype(o_ref.dtype)


def _attention(q, k, vt, *, t):
    b, s, _ = q.shape
    assert vt.shape == (b, s // t, ATTN_WIDTH, t)
    return pl.pallas_call(
        functools.partial(_attn_kernel, t=t),
        grid=(b, N_HEADS // 2, s // t),
        in_specs=[pl.BlockSpec((1, t, 2 * HEAD_PAD), lambda i, h, j: (i, j, h)),
                  pl.BlockSpec((1, s, 2 * HEAD_PAD), lambda i, h, j: (i, 0, h)),
                  pl.BlockSpec((1, s // t, 2 * V_HEAD_DIM, t), lambda i, h, j: (i, 0, h, 0))],
        out_specs=pl.BlockSpec((1, t, 2 * V_HEAD_DIM), lambda i, h, j: (i, j, h)),
        out_shape=jax.ShapeDtypeStruct((b, s, ATTN_WIDTH), BF16),
        scratch_shapes=[pltpu.VMEM((t, t), F32), pltpu.VMEM((t, t), F32)],
        compiler_params=_cparams(("arbitrary", "arbitrary", "arbitrary")),
        name="attn",
    )(q, k, vt)


def _post0_kernel(x_ref, u_ref, a_ref, wu_ref, wa_ref, o_ref):
    o_ref[...] = (x_ref[...]
                  + jnp.dot(u_ref[...], wu_ref[...], preferred_element_type=F32)
                  + jnp.dot(a_ref[...], wa_ref[...], preferred_element_type=F32))


def _post0(x, u, a, wu, wa, *, tm):
    n, d = x.shape
    row = lambda w: pl.BlockSpec((tm, w), lambda i: (i, 0))
    const = lambda shape: pl.BlockSpec(shape, lambda i: (0, 0))
    return pl.pallas_call(
        _post0_kernel,
        grid=(n // tm,),
        in_specs=[row(d), row(CONV_WIDTH), row(ATTN_WIDTH), const(wu.shape), const(wa.shape)],
        out_specs=row(d),
        out_shape=jax.ShapeDtypeStruct((n, d), F32),
        compiler_params=_cparams(("arbitrary",)),
        name="post0",
    )(x, u, a, wu, wa)


def _ffn_kernel(x_ref, g_ref, wup_ref, cw_ref, wd_ref, fg_ref,
                o_ref, xn_scr, hbuf0, hbuf1, carry, *, tm, final_norm):
    sb = pl.program_id(1)
    x = x_ref[0]
    xn_scr[...] = _rms(x, g_ref[...]).astype(BF16)
    o_ref[0] = x
    w2 = 2 * FFN_CHUNK

    def up(c, hbuf):
        h = jnp.dot(xn_scr[...], wup_ref[c], preferred_element_type=F32)
        hbuf[0:FFN_HALO, :] = jnp.where(sb == 0, 0.0, carry[c])
        hbuf[FFN_HALO:FFN_HALO + tm, :] = h
        carry[c] = h[tm - FFN_HALO:, :]

    def down(c, hbuf):
        w = cw_ref[c]
        cv = w[FFN_K - 1:FFN_K, :] * hbuf[FFN_HALO:FFN_HALO + tm, :] + w[FFN_K:FFN_K + 1, :]
        for k in range(FFN_K - 1):
            off = FFN_HALO - (FFN_K - 1) + k
            cv = cv + w[k:k + 1, :] * hbuf[off:off + tm, :]
        ca, cg = cv[:, :FFN_CHUNK], cv[:, FFN_CHUNK:]
        act = (ca * _sigmoid(ca) * cg).astype(BF16)
        o_ref[0] += jnp.dot(act, wd_ref[c], preferred_element_type=F32)

    up(0, hbuf0)

    def pair(i, _):
        c = 2 * i
        up(c + 1, hbuf1)
        down(c, hbuf0)
        up(c + 2, hbuf0)
        down(c + 1, hbuf1)
        return 0

    assert FFN_NCHUNK % 2 == 1
    lax.fori_loop(0, FFN_NCHUNK // 2, pair, 0)
    down(FFN_NCHUNK - 1, hbuf0)
    if final_norm:
        o_ref[0] = _rms(o_ref[0], fg_ref[...])


def _ffn(x, g, wup, cw, wd, fg, *, tm, final_norm):
    b, s, d = x.shape
    nc = FFN_NCHUNK
    const = lambda shape: pl.BlockSpec(shape, lambda i, j: (0,) * len(shape), pipeline_mode=pl.Buffered(1))
    return pl.pallas_call(
        functools.partial(_ffn_kernel, tm=tm, final_norm=final_norm),
        grid=(b, s // tm),
        in_specs=[pl.BlockSpec((1, tm, d), lambda i, j: (i, j, 0)),
                  const(g.shape), const(wup.shape), const(cw.shape), const(wd.shape), const(fg.shape)],
        out_specs=pl.BlockSpec((1, tm, d), lambda i, j: (i, j, 0)),
        out_shape=jax.ShapeDtypeStruct((b, s, d), F32),
        scratch_shapes=[pltpu.VMEM((tm, d), BF16),
                        pltpu.VMEM((tm + FFN_HALO, 2 * FFN_CHUNK), F32),
                        pltpu.VMEM((tm + FFN_HALO, 2 * FFN_CHUNK), F32),
                        pltpu.VMEM((nc, FFN_HALO, 2 * FFN_CHUNK), F32)],
        compiler_params=_cparams(("arbitrary", "arbitrary")),
        name="ffn_final" if final_norm else "ffn",
    )(x, g, wup, cw, wd, fg)


def _regroup(slabs, n_out, pieces):
    rows = slabs[0].shape[0]
    quarter = lax.broadcasted_iota(jnp.int32, (rows, LANES), 1) // SSM_PAIR_CH
    out = []
    for d in range(n_out):
        acc = None
        for dst_q, (src, src_q) in enumerate(pieces(d)):
            shift = ((dst_q - src_q) * SSM_PAIR_CH) % LANES
            piece = pltpu.roll(slabs[src], shift, axis=1) if shift else slabs[src]
            acc = piece if acc is None else jnp.where(quarter == dst_q, piece, acc)
        out.append(acc)
    return out


def _shift_rows(x, n, row):
    return jnp.where(row >= n, pltpu.roll(x, n, axis=0), 0.0)


def _mix1_kernel(x_ref, g_ref, win_ref, mb_ref, t_ref, mc_ref, lp_ref, d_ref, wglu_ref, bglu_ref,
                 o_ref, carry, u_slab, y_slab, *, rows, nlev):
    sb = pl.program_id(1)

    @pl.when(sb == 0)
    def _():
        carry[...] = jnp.zeros(carry.shape, F32)

    x = x_ref[0]
    xn = _rms(x, g_ref[...]).astype(BF16)
    u = jnp.dot(xn, win_ref[...], preferred_element_type=F32)
    nch = SSM_WIDTH // LANES
    for s in range(nch):
        u_slab[s] = u[:, s * LANES:(s + 1) * LANES]
    nslab = SSM_ROW // LANES
    u_slabs = [u_slab[i % nch, pl.ds(i // nch, rows, stride=SSM_L), :] for i in range(nslab)]
    a_slabs = _regroup(u_slabs, nslab,
                       lambda d: [(4 * (4 * (d % 2) + i) + (d // 2) // 4, (d // 2) % 4) for i in range(4)])
    row = lax.broadcasted_iota(jnp.int32, (rows, LANES), 0)
    first = row == 0
    y_slabs = []
    for p in range(SSM_PAIRS):
        a = jnp.concatenate(a_slabs[2 * p:2 * p + 2], axis=1)
        a16 = a.astype(BF16)
        lo = slice(p * SSM_TILE, p * SSM_TILE + LANES)
        hi = slice(p * SSM_TILE + LANES, (p + 1) * SSM_TILE)
        sloc = jnp.dot(a16, mb_ref[p], preferred_element_type=F32)
        sr, si = sloc[:, :LANES], sloc[:, LANES:]
        cr, ci = carry[0:1, lo], carry[0:1, hi]
        lr, li = lp_ref[0:1, lo], lp_ref[0:1, hi]
        sr = sr + jnp.where(first, lr * cr - li * ci, 0.0)
        si = si + jnp.where(first, lr * ci + li * cr, 0.0)
        for lev in range(nlev):
            n = 1 << lev
            lr, li = lp_ref[lev:lev + 1, lo], lp_ref[lev:lev + 1, hi]
            pr, pi = _shift_rows(sr, n, row), _shift_rows(si, n, row)
            sr, si = sr + (lr * pr - li * pi), si + (lr * pi + li * pr)
        xr = jnp.where(first, cr, pltpu.roll(sr, 1, axis=0))
        xi = jnp.where(first, ci, pltpu.roll(si, 1, axis=0))
        carry[0:1, lo] = sr[rows - 1:rows, :]
        carry[0:1, hi] = si[rows - 1:rows, :]
        xp = jnp.concatenate([xr, xi], axis=1).astype(BF16)
        y = (jnp.dot(a16, t_ref[p], preferred_element_type=F32)
             + jnp.dot(xp, mc_ref[p], preferred_element_type=F32)
             + d_ref[:, p * SSM_TILE:(p + 1) * SSM_TILE] * a)
        y = jax.nn.gelu(y, approximate=True)
        y_slabs += [y[:, :LANES], y[:, LANES:]]
    o_slabs = _regroup(y_slabs, nslab,
                       lambda s: [(2 * (4 * (s % 4) + j) + (s // 4) // 4, (s // 4) % 4) for j in range(4)])
    for i in range(nslab):
        y_slab[i % nch, pl.ds(i // nch, rows, stride=SSM_L), :] = o_slabs[i]
    y = jnp.concatenate([y_slab[s] for s in range(nch)], axis=1).astype(BF16)
    z = jnp.dot(y, wglu_ref[...], preferred_element_type=F32) + bglu_ref[...]
    o_ref[0] = x + z[:, :D_MODEL] * _sigmoid(z[:, D_MODEL:])


def _mix1(x, g, win, mb, tt, mc, lp, dd, wglu, bglu, *, rows):
    b, s, d = x.shape
    tok = rows * SSM_L
    nlev = int(math.log2(rows))
    assert 1 << nlev == rows and lp.shape[0] >= nlev
    const = lambda a: pl.BlockSpec(a.shape, lambda i, j: (0,) * a.ndim, pipeline_mode=pl.Buffered(1))
    return pl.pallas_call(
        functools.partial(_mix1_kernel, rows=rows, nlev=nlev),
        grid=(b, s // tok),
        in_specs=[pl.BlockSpec((1, tok, d), lambda i, j: (i, j, 0)),
                  const(g), const(win), const(mb), const(tt), const(mc), const(lp), const(dd),
                  const(wglu), const(bglu)],
        out_specs=pl.BlockSpec((1, tok, d), lambda i, j: (i, j, 0)),
        out_shape=jax.ShapeDtypeStruct((b, s, d), F32),
        scratch_shapes=[pltpu.VMEM((SUBLANES, SSM_ROW), F32),
                        pltpu.VMEM((SSM_WIDTH // LANES, tok, LANES), F32),
                        pltpu.VMEM((SSM_WIDTH // LANES, tok, LANES), F32)],
        compiler_params=_cparams(("arbitrary", "arbitrary")),
        name="mix1",
    )(x, g, win, mb, tt, mc, lp, dd, wglu, bglu)


def _ssm_tables(log_dt, a_re, a_im, b_re, b_im, c_re, c_im, d_skip, nlev):
    g, p, c, L = SSM_GROUPS, SSM_STATE, SSM_GROUP, SSM_L
    dt = jnp.exp(log_dt.astype(F32))[:, None]
    ar, ai = a_re.astype(F32), a_im.astype(F32)
    mag = jnp.exp(ar * dt)
    lb_re, lb_im = mag * jnp.cos(ai * dt), mag * jnp.sin(ai * dt)
    den = ar * ar + ai * ai
    nr, ni = lb_re - 1.0, lb_im
    f_re = (nr * ar + ni * ai) / den
    f_im = (ni * ar - nr * ai) / den
    br, bi = b_re.astype(F32), b_im.astype(F32)
    bb_re = f_re[..., None] * br - f_im[..., None] * bi
    bb_im = f_re[..., None] * bi + f_im[..., None] * br

    def lam_pow(n):
        n = jnp.asarray(n, F32)[:, None, None]
        m = jnp.exp(n * (ar * dt))
        return m * jnp.cos(n * (ai * dt)), m * jnp.sin(n * (ai * dt))

    cr, ci = c_re.astype(F32), c_im.astype(F32)
    pr, pi = lam_pow(jnp.arange(L + 1))
    lbr = pr[..., None] * bb_re[None] - pi[..., None] * bb_im[None]
    lbi = pr[..., None] * bb_im[None] + pi[..., None] * bb_re[None]
    kk = (jnp.einsum('gop,ngpc->ngoc', cr, lbr[:L]) - jnp.einsum('gop,ngpc->ngoc', ci, lbi[:L]))
    cat = jnp.concatenate
    q = SSM_PAIRS

    def pair_diag(a0, a1):
        z = jnp.zeros_like(a0)
        return cat([cat([a0, z], axis=2), cat([z, a1], axis=2)], axis=1)

    kt = kk.transpose(0, 1, 3, 2).reshape(L, q, 2, c, c)
    kbd = [pair_diag(kt[n, :, 0], kt[n, :, 1]) for n in range(L)]
    zero = jnp.zeros_like(kbd[0])
    tt = cat([cat([kbd[t2 - t1] if t2 >= t1 else zero for t2 in range(L)], axis=2) for t1 in range(L)], axis=1)
    def mb_rows(n):
        r = lbr[n].transpose(0, 2, 1).reshape(q, 2, c, p)
        i = lbi[n].transpose(0, 2, 1).reshape(q, 2, c, p)
        return cat([pair_diag(r[:, 0], r[:, 1]), pair_diag(i[:, 0], i[:, 1])], axis=2)
    mb = cat([mb_rows(L - 1 - t) for t in range(L)], axis=1)
    p1r, p1i = pr[1:], pi[1:]
    clr = cr[None] * p1r[:, :, None, :] - ci[None] * p1i[:, :, None, :]
    cli = cr[None] * p1i[:, :, None, :] + ci[None] * p1r[:, :, None, :]
    def mc_cols(t):
        r = clr[t].transpose(0, 2, 1).reshape(q, 2, p, c)
        i = -cli[t].transpose(0, 2, 1).reshape(q, 2, p, c)
        return cat([pair_diag(r[:, 0], r[:, 1]), pair_diag(i[:, 0], i[:, 1])], axis=1)
    mc = cat([mc_cols(t) for t in range(L)], axis=2)
    sr, si = lam_pow(L * (2 ** jnp.arange(nlev)))
    lp = jnp.stack([sr.reshape(nlev, SSM_PAIRS, 2 * p), si.reshape(nlev, SSM_PAIRS, 2 * p)], axis=2)
    lp = lp.reshape(nlev, SSM_ROW)
    lp = jnp.pad(lp, ((0, (-nlev) % SUBLANES), (0, 0)))
    dd = jnp.broadcast_to(d_skip.astype(F32).reshape(SSM_PAIRS, 1, SSM_PAIR_CH),
                          (SSM_PAIRS, L, SSM_PAIR_CH)).reshape(1, SSM_ROW)
    return mb.astype(BF16), tt.astype(BF16), mc.astype(BF16), lp, dd


def _head_pad_cols(nope, rope):
    k = nope.shape[0]
    parts = [nope]
    if rope is not None:
        parts.append(rope)
    used = sum(t.shape[-1] for t in parts)
    parts.append(jnp.zeros((k, N_HEADS, HEAD_PAD - used), nope.dtype))
    return jnp.concatenate(parts, axis=-1).reshape(k, QK_WIDTH)


def _rope_tables(s):
    inv = ROPE_BASE ** (-jnp.arange(ROPE_HALF, dtype=F32) / ROPE_HALF)
    ang = jnp.arange(s, dtype=F32)[:, None] * inv[None, :]
    cos, sin = jnp.cos(ang), jnp.sin(ang)
    gap = jnp.zeros((s, QK_NOPE_DIM - QK_ROPE_DIM), F32)
    tail = jnp.ones((s, HEAD_PAD - QK_NOPE_DIM - QK_ROPE_DIM), F32)
    return jnp.concatenate([sin, -sin, gap, cos, cos, tail], axis=1)


def _pick(total, want):
    t = min(total, want)
    assert total % t == 0
    return t


def kernel(x, l0_mix_norm, l0_w_in, l0_conv_w, l0_conv_b, l0_conv_ln_g, l0_conv_ln_b, l0_q_norm, l0_kv_norm, l0_w_uq, l0_w_ukv, l0_w_out, l0_ffn_norm, l0_w_up, l0_ffn_conv_w, l0_ffn_conv_b, l0_w_down, l1_mix_norm, l1_w_in, l1_log_dt, l1_a_re, l1_a_im, l1_b_re, l1_b_im, l1_c_re, l1_c_im, l1_d, l1_w_glu, l1_b_glu, l1_ffn_norm, l1_w_up, l1_ffn_conv_w, l1_ffn_conv_b, l1_w_down, final_norm):
    b, s, d = x.shape
    n = b * s
    row2 = lambda v: v.reshape(1, -1).astype(F32)

    o = 2 * CONV_WIDTH + Q_LORA_RANK + KV_LORA_RANK
    w_kr = l0_w_in[:, o:]
    kr_blk = jnp.concatenate([jnp.zeros((d, QK_NOPE_DIM), F32), w_kr,
                              jnp.zeros((d, HEAD_PAD - QK_NOPE_DIM - QK_ROPE_DIM), F32)], axis=1)
    w_in0 = jnp.concatenate([l0_w_in[:, :o], kr_blk], axis=1).astype(BF16)
    scale = (QK_NOPE_DIM + QK_ROPE_DIM) ** -0.5 * math.log2(math.e)
    wq = (l0_w_uq * scale).reshape(Q_LORA_RANK, N_HEADS, QK_NOPE_DIM + QK_ROPE_DIM)
    wuq = _head_pad_cols(wq[..., :QK_NOPE_DIM], wq[..., QK_NOPE_DIM:]).astype(BF16)
    wkv = l0_w_ukv.reshape(KV_LORA_RANK, N_HEADS, QK_NOPE_DIM + V_HEAD_DIM)
    wukv = jnp.concatenate([_head_pad_cols(wkv[..., :QK_NOPE_DIM], None),
                            wkv[..., QK_NOPE_DIM:].reshape(KV_LORA_RANK, ATTN_WIDTH)], axis=1).astype(BF16)
    cw0 = jnp.pad(l0_conv_w.astype(F32), ((0, CONV_HALO - CONV_K), (0, 0)))
    rope = _rope_tables(s)

    t_attn = _pick(s, 512)
    u0, q, k, vt = _pre0(x, row2(l0_mix_norm), w_in0, rope, cw0, row2(l0_conv_b),
                         row2(l0_conv_ln_g), row2(l0_conv_ln_b), row2(l0_q_norm), row2(l0_kv_norm),
                         wuq, wukv, tm=t_attn)
    attn = _attention(q, k, vt, t=t_attn)
    w_out = l0_w_out.astype(BF16)
    x1 = _post0(x.reshape(n, d), u0.reshape(n, CONV_WIDTH), attn.reshape(n, ATTN_WIDTH),
                w_out[:CONV_WIDTH], w_out[CONV_WIDTH:], tm=_pick(n, 1024)).reshape(b, s, d)

    def ffn(xx, g, w_up, cw, cb, w_down, final):
        chunked = lambda w: (w.reshape(w.shape[0], 2, FFN_NCHUNK, FFN_CHUNK).transpose(2, 0, 1, 3)
                             .reshape(FFN_NCHUNK, w.shape[0], 2 * FFN_CHUNK))
        taps = jnp.concatenate([cw.astype(F32), cb.astype(F32)[None],
                                jnp.zeros((SUBLANES - FFN_K - 1, 2 * D_FF), F32)], axis=0)
        return _ffn(xx, row2(g), chunked(w_up.astype(BF16)), chunked(taps),
                    w_down.astype(BF16).reshape(FFN_NCHUNK, FFN_CHUNK, d),
                    row2(final_norm), tm=_pick(s, 512), final_norm=final)

    x2 = ffn(x1, l0_ffn_norm, l0_w_up, l0_ffn_conv_w, l0_ffn_conv_b, l0_w_down, False)

    rows = _pick(s // SSM_L, 128)
    nlev = int(math.log2(rows))
    mb, tt, mc, lp, dd = _ssm_tables(l1_log_dt, l1_a_re, l1_a_im, l1_b_re, l1_b_im,
                                     l1_c_re, l1_c_im, l1_d, nlev)
    x3 = _mix1(x2, row2(l1_mix_norm), l1_w_in.astype(BF16), mb, tt, mc, lp, dd,
               l1_w_glu.astype(BF16), row2(l1_b_glu), rows=rows)
    return ffn(x3, l1_ffn_norm, l1_w_up, l1_ffn_conv_w, l1_ffn_conv_b, l1_w_down, True)
```

```python
import functools
import math

import jax
import jax.numpy as jnp
from jax import lax
from jax.experimental import pallas as pl
from jax.experimental.pallas import tpu as pltpu

F32 = jnp.float32
BF16 = jnp.bfloat16

D_MODEL = 1024
EPS = 1e-6
LN_EPS = 1e-5
CONV_WIDTH = 512
CONV_K = 31
N_HEADS = 8
QK_NOPE_DIM = 64
QK_ROPE_DIM = 32
ROPE_HALF = QK_ROPE_DIM // 2
V_HEAD_DIM = 64
Q_LORA_RANK = 256
KV_LORA_RANK = 128
ROPE_BASE = 10000.0
ATTN_WIDTH = N_HEADS * V_HEAD_DIM
SSM_WIDTH = 512
SSM_GROUP = 16
SSM_GROUPS = SSM_WIDTH // SSM_GROUP
SSM_STATE = 64
D_FF = 2816
FFN_K = 3

LANES = 128
SUBLANES = 8
V_ROWS = V_HEAD_DIM + 16
HEAD_PAD = LANES
QK_WIDTH = N_HEADS * HEAD_PAD
CONV_HALO = 32
CONV_ROWS = 64
FFN_HALO = SUBLANES
FFN_CHUNK = 256
FFN_NCHUNK = D_FF // FFN_CHUNK
SSM_L = 8
SSM_PAIRS = SSM_GROUPS // 2
SSM_PAIR_CH = 2 * SSM_GROUP
SSM_TILE = SSM_L * SSM_PAIR_CH
SSM_ROW = SSM_L * SSM_WIDTH
VMEM_LIMIT = 56 * 1024 * 1024


def _cparams(sem):
    return pltpu.CompilerParams(dimension_semantics=sem, vmem_limit_bytes=VMEM_LIMIT)


def _rms(x, g):
    return x * lax.rsqrt(jnp.mean(x * x, axis=-1, keepdims=True) + EPS) * g


def _sigmoid(x):
    return 1.0 / (1.0 + jnp.exp(-x))


def _rope128(t, cos, sin_up, sin_dn):
    return (t * cos + pltpu.roll(t, ROPE_HALF, axis=1) * sin_up
            + pltpu.roll(t, LANES - ROPE_HALF, axis=1) * sin_dn)


def _rope_unpack(packed):
    lane = lax.broadcasted_iota(jnp.int32, packed.shape, 1)
    r1, r2 = QK_NOPE_DIM, QK_NOPE_DIM + ROPE_HALF
    cos = jnp.where(lane < r1, 1.0, packed)
    up = pltpu.roll(packed, r2, axis=1)
    dn = pltpu.roll(packed, r1 - ROPE_HALF, axis=1)
    sin_up = jnp.where(lane >= r2, jnp.where(lane < r2 + ROPE_HALF, up, 0.0), 0.0)
    sin_dn = jnp.where(lane >= r1, jnp.where(lane < r2, dn, 0.0), 0.0)
    return cos, sin_up, sin_dn


def _pre0_kernel(x_ref, g_ref, w_in_ref, rope_ref, cw_ref, cb_ref, lng_ref, lnb_ref,
                 qn_ref, kvn_ref, wuq_ref, wukv_ref,
                 u_ref, q_ref, k_ref, vt_ref, ubuf, ushift, h_scr, *, tm):
    sb = pl.program_id(1)

    @pl.when(sb == 0)
    def _():
        ubuf[0:CONV_HALO, :] = jnp.zeros((CONV_HALO, CONV_WIDTH), F32)

    hm = tm // 2
    hs = []
    for half in range(2):
        xn = _rms(x_ref[0, half * hm:(half + 1) * hm, :], g_ref[...]).astype(BF16)
        h_scr[half] = jnp.dot(xn, w_in_ref[...], preferred_element_type=F32)
        hs.append(h_scr.at[half])

    base = CONV_HALO - (CONV_K - 1)
    for half, h in enumerate(hs):
        t0 = half * hm
        rows = slice(t0, t0 + hm)
        ubuf[CONV_HALO + t0:CONV_HALO + t0 + hm, :] = h[:, :CONV_WIDTH] * _sigmoid(h[:, CONV_WIDTH:2 * CONV_WIDTH])
        span = hm + CONV_HALO - SUBLANES
        for r in range(1, SUBLANES):
            ushift[r - 1, t0:t0 + span, :] = ubuf[t0 + r:t0 + r + span, :]
        for r0 in range(t0, t0 + hm, CONV_ROWS):
            acc = jnp.broadcast_to(cb_ref[...], (CONV_ROWS, CONV_WIDTH))
            for o in range(base, base + CONV_K):
                r, al = o % SUBLANES, o - o % SUBLANES + r0
                tap = ushift[r - 1, al:al + CONV_ROWS, :] if r else ubuf[al:al + CONV_ROWS, :]
                acc = acc + cw_ref[o - base:o - base + 1, :] * tap
            mu = jnp.mean(acc, axis=-1, keepdims=True)
            cen = acc - mu
            var = jnp.mean(cen * cen, axis=-1, keepdims=True)
            y = cen * lax.rsqrt(var + LN_EPS) * lng_ref[...] + lnb_ref[...]
            u_ref[0, r0:r0 + CONV_ROWS, :] = (y * _sigmoid(y)).astype(BF16)

        cos, sin_up, sin_dn = _rope_unpack(rope_ref[rows, :])
        o = 2 * CONV_WIDTH
        cq = _rms(h[:, o:o + Q_LORA_RANK], qn_ref[...]).astype(BF16)
        q = jnp.dot(cq, wuq_ref[...], preferred_element_type=F32)
        o += Q_LORA_RANK
        ckv = _rms(h[:, o:o + KV_LORA_RANK], kvn_ref[...]).astype(BF16)
        kv = jnp.dot(ckv, wukv_ref[...], preferred_element_type=F32)
        o += KV_LORA_RANK
        kr = _rope128(h[:, o:o + LANES], cos, sin_up, sin_dn)
        for hd in range(N_HEADS):
            sl = slice(hd * HEAD_PAD, (hd + 1) * HEAD_PAD)
            q_ref[0, rows, sl] = _rope128(q[:, sl], cos, sin_up, sin_dn).astype(BF16)
            k_ref[0, rows, sl] = (kv[:, sl] + kr).astype(BF16)
        vt = kv[:, QK_WIDTH:].T
        ones = jnp.ones((V_ROWS - V_HEAD_DIM, hm), F32)
        parts = [blk for hd in range(N_HEADS) for blk in (vt[hd * V_HEAD_DIM:(hd + 1) * V_HEAD_DIM], ones)]
        vt_ref[0, 0, :, rows] = jnp.concatenate(parts, axis=0).astype(BF16)
    ubuf[0:CONV_HALO, :] = ubuf[tm:tm + CONV_HALO, :]


def _pre0(x, g, w_in, rope, cw, cb, lng, lnb, qn, kvn, wuq, wukv, *, tm):
    b, s, d = x.shape
    const = lambda shape: pl.BlockSpec(shape, lambda i, j: (0,) * len(shape))
    row = lambda w: pl.BlockSpec((1, tm, w), lambda i, j: (i, j, 0))
    return pl.pallas_call(
        functools.partial(_pre0_kernel, tm=tm),
        grid=(b, s // tm),
        in_specs=[row(d), const(g.shape), const(w_in.shape),
                  pl.BlockSpec((tm, LANES), lambda i, j: (j, 0)),
                  const(cw.shape), const(cb.shape), const(lng.shape), const(lnb.shape),
                  const(qn.shape), const(kvn.shape), const(wuq.shape), const(wukv.shape)],
        out_specs=[row(CONV_WIDTH), row(QK_WIDTH), row(QK_WIDTH),
                   pl.BlockSpec((1, 1, N_HEADS * V_ROWS, tm), lambda i, j: (i, j, 0, 0))],
        out_shape=[jax.ShapeDtypeStruct((b, s, CONV_WIDTH), BF16),
                   jax.ShapeDtypeStruct((b, s, QK_WIDTH), BF16),
                   jax.ShapeDtypeStruct((b, s, QK_WIDTH), BF16),
                   jax.ShapeDtypeStruct((b, s // tm, N_HEADS * V_ROWS, tm), BF16)],
        scratch_shapes=[pltpu.VMEM((tm + CONV_HALO, CONV_WIDTH), F32),
                        pltpu.VMEM((SUBLANES - 1, tm + CONV_HALO, CONV_WIDTH), F32),
                        pltpu.VMEM((2, tm // 2, w_in.shape[1]), F32)],
        compiler_params=_cparams(("arbitrary", "arbitrary")),
        name="pre0",
    )(x, g, w_in, rope, cw, cb, lng, lnb, qn, kvn, wuq, wukv)


NEG_BIG = -1e30


def _attn_kernel(q_ref, k_ref, vt_ref, o_ref, s0_ref, s1_ref, *, t):
    qi = pl.program_id(2)
    s_refs = (s0_ref, s1_ref)

    def scores(j, hh):
        lanes = slice(hh * HEAD_PAD, (hh + 1) * HEAD_PAD)
        k = k_ref[0, pl.ds(pl.multiple_of(j * t, t), t), lanes]
        s_refs[hh][...] = lax.dot_general(k, q_ref[0, :, lanes], (((1,), (1,)), ((), ())),
                                          preferred_element_type=F32)

    def update(j, hh, state, masked):
        m, acc = state
        st = s_refs[hh][...]
        if masked:
            key = lax.broadcasted_iota(jnp.int32, (t, t), 0)
            qry = lax.broadcasted_iota(jnp.int32, (t, t), 1)
            st = jnp.where(key <= qry, st, NEG_BIG)
        m_new = jnp.maximum(m, jnp.max(st, axis=0, keepdims=True))
        alpha = jnp.exp2(m - m_new)
        p = jnp.exp2(st - m_new).astype(BF16)
        vt = vt_ref[0, j, hh * V_ROWS:(hh + 1) * V_ROWS, :]
        acc = alpha * acc + jnp.dot(vt, p, preferred_element_type=F32)
        return m_new, acc

    def body(j, carry):
        scores(j, 1)
        st0 = update(j, 0, carry[:2], False)
        scores(j + 1, 0)
        st1 = update(j, 1, carry[2:], False)
        return st0 + st1

    init = (jnp.full((1, t), NEG_BIG, F32), jnp.zeros((V_ROWS, t), F32)) * 2
    scores(0, 0)
    carry = lax.fori_loop(0, qi // 2, lambda i, c: body(2 * i + 1, body(2 * i, c)), init)
    carry = lax.fori_loop(qi // 2 * 2, qi, body, carry)
    scores(qi, 1)
    _, acc0 = update(qi, 0, carry[:2], True)
    _, acc1 = update(qi, 1, carry[2:], True)
    d = V_HEAD_DIM
    ot = jnp.concatenate([acc0[:d] / acc0[d:d + 1], acc1[:d] / acc1[d:d + 1]], axis=0)
    o_ref[0] = ot.T.astype(o_ref.dtype)


def _attention(q, k, vt, *, t):
    b, s, _ = q.shape
    assert vt.shape == (b, s // t, N_HEADS * V_ROWS, t)
    return pl.pallas_call(
        functools.partial(_attn_kernel, t=t),
        grid=(b, N_HEADS // 2, s // t),
        in_specs=[pl.BlockSpec((1, t, 2 * HEAD_PAD), lambda i, h, j: (i, j, h)),
                  pl.BlockSpec((1, s, 2 * HEAD_PAD), lambda i, h, j: (i, 0, h)),
                  pl.BlockSpec((1, s // t, 2 * V_ROWS, t), lambda i, h, j: (i, 0, h, 0))],
        out_specs=pl.BlockSpec((1, t, 2 * V_HEAD_DIM), lambda i, h, j: (i, j, h)),
        out_shape=jax.ShapeDtypeStruct((b, s, ATTN_WIDTH), BF16),
        scratch_shapes=[pltpu.VMEM((t, t), F32), pltpu.VMEM((t, t), F32)],
        compiler_params=_cparams(("arbitrary", "arbitrary", "arbitrary")),
        name="attn",
    )(q, k, vt)


def _ffn_kernel(*refs, tm, mix_out, final_norm):
    if mix_out:
        x_ref, u_ref, a_ref, wu_ref, wa_ref, *refs = refs
    else:
        x_ref, *refs = refs
    g_ref, wup_ref, cw_ref, wd_ref, fg_ref, o_ref, xn_scr, hbuf0, hbuf1, carry = refs
    sb = pl.program_id(1)
    x = x_ref[0]
    if mix_out:
        x = (x + jnp.dot(u_ref[0], wu_ref[...], preferred_element_type=F32)
             + jnp.dot(a_ref[0], wa_ref[...], preferred_element_type=F32))
    xn_scr[...] = _rms(x, g_ref[...]).astype(BF16)
    o_ref[0] = x
    halves = lambda c: ((slice(0, FFN_CHUNK), c * FFN_CHUNK),
                        (slice(FFN_CHUNK, 2 * FFN_CHUNK), D_FF + c * FFN_CHUNK))

    def up(c, hbuf):
        for lanes, col in halves(c):
            h = jnp.dot(xn_scr[...], wup_ref[:, col:col + FFN_CHUNK], preferred_element_type=F32)
            hbuf[0:FFN_HALO, lanes] = jnp.where(sb == 0, 0.0, carry[c, :, lanes])
            hbuf[FFN_HALO:FFN_HALO + tm, lanes] = h
            carry[c, :, lanes] = h[tm - FFN_HALO:, :]

    def down(c, hbuf):
        cv = []
        for lanes, col in halves(c):
            w = cw_ref[:, col:col + FFN_CHUNK]
            v = w[FFN_K - 1:FFN_K, :] * hbuf[FFN_HALO:FFN_HALO + tm, lanes] + w[FFN_K:FFN_K + 1, :]
            for k in range(FFN_K - 1):
                off = FFN_HALO - (FFN_K - 1) + k
                v = v + w[k:k + 1, :] * hbuf[off:off + tm, lanes]
            cv.append(v)
        act = (cv[0] * _sigmoid(cv[0]) * cv[1]).astype(BF16)
        o_ref[0] += jnp.dot(act, wd_ref[c * FFN_CHUNK:(c + 1) * FFN_CHUNK, :], preferred_element_type=F32)

    bufs = (hbuf0, hbuf1)
    up(0, bufs[0])
    for c in range(FFN_NCHUNK):
        if c + 1 < FFN_NCHUNK:
            up(c + 1, bufs[(c + 1) % 2])
        down(c, bufs[c % 2])
    if final_norm:
        o_ref[0] = _rms(o_ref[0], fg_ref[...])


def _ffn(x, mix, g, wup, cw, wd, fg, *, tm, final_norm):
    b, s, d = x.shape
    const = lambda a: pl.BlockSpec(a.shape, lambda i, j: (0,) * a.ndim, pipeline_mode=pl.Buffered(1))
    row = lambda w: pl.BlockSpec((1, tm, w), lambda i, j: (i, j, 0))
    ins, specs = [x], [row(d)]
    if mix is not None:
        u, a, wu, wa = mix
        ins += [u, a, wu, wa]
        specs += [row(u.shape[-1]), row(a.shape[-1]), const(wu), const(wa)]
    ins += [g, wup, cw, wd, fg]
    specs += [const(g), const(wup), const(cw), const(wd), const(fg)]
    return pl.pallas_call(
        functools.partial(_ffn_kernel, tm=tm, mix_out=mix is not None, final_norm=final_norm),
        grid=(b, s // tm),
        in_specs=specs,
        out_specs=row(d),
        out_shape=jax.ShapeDtypeStruct((b, s, d), F32),
        scratch_shapes=[pltpu.VMEM((tm, d), BF16),
                        pltpu.VMEM((tm + FFN_HALO, 2 * FFN_CHUNK), F32),
                        pltpu.VMEM((tm + FFN_HALO, 2 * FFN_CHUNK), F32),
                        pltpu.VMEM((FFN_NCHUNK, FFN_HALO, 2 * FFN_CHUNK), F32)],
        compiler_params=_cparams(("arbitrary", "arbitrary")),
        name="ffn_final" if final_norm else "ffn",
    )(*ins)


def _regroup(slabs, n_out, pieces):
    rows = slabs[0].shape[0]
    quarter = lax.broadcasted_iota(jnp.int32, (rows, LANES), 1) // SSM_PAIR_CH
    out = []
    for d in range(n_out):
        acc = None
        for dst_q, (src, src_q) in enumerate(pieces(d)):
            shift = ((dst_q - src_q) * SSM_PAIR_CH) % LANES
            piece = pltpu.roll(slabs[src], shift, axis=1) if shift else slabs[src]
            acc = piece if acc is None else jnp.where(quarter == dst_q, piece, acc)
        out.append(acc)
    return out


def _shift_rows(x, n, row):
    return jnp.where(row >= n, pltpu.roll(x, n, axis=0), 0.0)


def _mix1_kernel(x_ref, g_ref, win_ref, mb_ref, t_ref, mc_ref, lp_ref, d_ref, wglu_ref, bglu_ref,
                 o_ref, carry, u_slab, y_slab, *, rows, nlev):
    sb = pl.program_id(1)

    @pl.when(sb == 0)
    def _():
        carry[...] = jnp.zeros(carry.shape, F32)

    x = x_ref[0]
    xn = _rms(x, g_ref[...]).astype(BF16)
    u = jnp.dot(xn, win_ref[...], preferred_element_type=F32)
    nch = SSM_WIDTH // LANES
    for s in range(nch):
        u_slab[s] = u[:, s * LANES:(s + 1) * LANES]
    nslab = SSM_ROW // LANES
    u_slabs = [u_slab[i % nch, pl.ds(i // nch, rows, stride=SSM_L), :] for i in range(nslab)]
    a_slabs = _regroup(u_slabs, nslab,
                       lambda d: [(4 * (4 * (d % 2) + i) + (d // 2) // 4, (d // 2) % 4) for i in range(4)])
    row = lax.broadcasted_iota(jnp.int32, (rows, LANES), 0)
    first = row == 0
    y_slabs = []
    for p in range(SSM_PAIRS):
        a = jnp.concatenate(a_slabs[2 * p:2 * p + 2], axis=1)
        a16 = a.astype(BF16)
        lo = slice(p * SSM_TILE, p * SSM_TILE + LANES)
        hi = slice(p * SSM_TILE + LANES, (p + 1) * SSM_TILE)
        sloc = jnp.dot(a16, mb_ref[p], preferred_element_type=F32)
        sr, si = sloc[:, :LANES], sloc[:, LANES:]
        cr, ci = carry[0:1, lo], carry[0:1, hi]
        lr, li = lp_ref[0:1, lo], lp_ref[0:1, hi]
        sr = sr + jnp.where(first, lr * cr - li * ci, 0.0)
        si = si + jnp.where(first, lr * ci + li * cr, 0.0)
        for lev in range(nlev):
            n = 1 << lev
            lr, li = lp_ref[lev:lev + 1, lo], lp_ref[lev:lev + 1, hi]
            pr, pi = _shift_rows(sr, n, row), _shift_rows(si, n, row)
            sr, si = sr + (lr * pr - li * pi), si + (lr * pi + li * pr)
        xr = jnp.where(first, cr, pltpu.roll(sr, 1, axis=0))
        xi = jnp.where(first, ci, pltpu.roll(si, 1, axis=0))
        carry[0:1, lo] = sr[rows - 1:rows, :]
        carry[0:1, hi] = si[rows - 1:rows, :]
        xp = jnp.concatenate([xr, xi], axis=1).astype(BF16)
        y = (jnp.dot(a16, t_ref[p], preferred_element_type=F32)
             + jnp.dot(xp, mc_ref[p], preferred_element_type=F32)
             + d_ref[:, p * SSM_TILE:(p + 1) * SSM_TILE] * a)
        y = jax.nn.gelu(y, approximate=True)
        y_slabs += [y[:, :LANES], y[:, LANES:]]
    o_slabs = _regroup(y_slabs, nslab,
                       lambda s: [(2 * (4 * (s % 4) + j) + (s // 4) // 4, (s // 4) % 4) for j in range(4)])
    for i in range(nslab):
        y_slab[i % nch, pl.ds(i // nch, rows, stride=SSM_L), :] = o_slabs[i]
    y = jnp.concatenate([y_slab[s] for s in range(nch)], axis=1).astype(BF16)
    z = jnp.dot(y, wglu_ref[...], preferred_element_type=F32) + bglu_ref[...]
    o_ref[0] = x + z[:, :D_MODEL] * _sigmoid(z[:, D_MODEL:])


def _mix1(x, g, win, mb, tt, mc, lp, dd, wglu, bglu, *, rows):
    b, s, d = x.shape
    tok = rows * SSM_L
    nlev = int(math.log2(rows))
    assert 1 << nlev == rows and lp.shape[0] >= nlev
    const = lambda a: pl.BlockSpec(a.shape, lambda i, j: (0,) * a.ndim, pipeline_mode=pl.Buffered(1))
    return pl.pallas_call(
        functools.partial(_mix1_kernel, rows=rows, nlev=nlev),
        grid=(b, s // tok),
        in_specs=[pl.BlockSpec((1, tok, d), lambda i, j: (i, j, 0)),
                  const(g), const(win), const(mb), const(tt), const(mc), const(lp), const(dd),
                  const(wglu), const(bglu)],
        out_specs=pl.BlockSpec((1, tok, d), lambda i, j: (i, j, 0)),
        out_shape=jax.ShapeDtypeStruct((b, s, d), F32),
        scratch_shapes=[pltpu.VMEM((SUBLANES, SSM_ROW), F32),
                        pltpu.VMEM((SSM_WIDTH // LANES, tok, LANES), F32),
                        pltpu.VMEM((SSM_WIDTH // LANES, tok, LANES), F32)],
        compiler_params=_cparams(("arbitrary", "arbitrary")),
        name="mix1",
    )(x, g, win, mb, tt, mc, lp, dd, wglu, bglu)


def _ssm_tables(log_dt, a_re, a_im, b_re, b_im, c_re, c_im, d_skip, nlev):
    g, p, c, L = SSM_GROUPS, SSM_STATE, SSM_GROUP, SSM_L
    dt = jnp.exp(log_dt.astype(F32))[:, None]
    ar, ai = a_re.astype(F32), a_im.astype(F32)
    mag = jnp.exp(ar * dt)
    lb_re, lb_im = mag * jnp.cos(ai * dt), mag * jnp.sin(ai * dt)
    den = ar * ar + ai * ai
    nr, ni = lb_re - 1.0, lb_im
    f_re = (nr * ar + ni * ai) / den
    f_im = (ni * ar - nr * ai) / den
    br, bi = b_re.astype(F32), b_im.astype(F32)
    bb_re = f_re[..., None] * br - f_im[..., None] * bi
    bb_im = f_re[..., None] * bi + f_im[..., None] * br

    def lam_pow(n):
        n = jnp.asarray(n, F32)[:, None, None]
        m = jnp.exp(n * (ar * dt))
        return m * jnp.cos(n * (ai * dt)), m * jnp.sin(n * (ai * dt))

    cr, ci = c_re.astype(F32), c_im.astype(F32)
    pr, pi = lam_pow(jnp.arange(L + 1))
    lbr = pr[..., None] * bb_re[None] - pi[..., None] * bb_im[None]
    lbi = pr[..., None] * bb_im[None] + pi[..., None] * bb_re[None]
    kk = (jnp.einsum('gop,ngpc->ngoc', cr, lbr[:L]) - jnp.einsum('gop,ngpc->ngoc', ci, lbi[:L]))
    cat = jnp.concatenate
    q = SSM_PAIRS

    def pair_diag(a0, a1):
        z = jnp.zeros_like(a0)
        return cat([cat([a0, z], axis=2), cat([z, a1], axis=2)], axis=1)

    kt = kk.transpose(0, 1, 3, 2).reshape(L, q, 2, c, c)
    kbd = [pair_diag(kt[n, :, 0], kt[n, :, 1]) for n in range(L)]
    zero = jnp.zeros_like(kbd[0])
    tt = cat([cat([kbd[t2 - t1] if t2 >= t1 else zero for t2 in range(L)], axis=2) for t1 in range(L)], axis=1)
    def mb_rows(n):
        r = lbr[n].transpose(0, 2, 1).reshape(q, 2, c, p)
        i = lbi[n].transpose(0, 2, 1).reshape(q, 2, c, p)
        return cat([pair_diag(r[:, 0], r[:, 1]), pair_diag(i[:, 0], i[:, 1])], axis=2)
    mb = cat([mb_rows(L - 1 - t) for t in range(L)], axis=1)
    p1r, p1i = pr[1:], pi[1:]
    clr = cr[None] * p1r[:, :, None, :] - ci[None] * p1i[:, :, None, :]
    cli = cr[None] * p1i[:, :, None, :] + ci[None] * p1r[:, :, None, :]
    def mc_cols(t):
        r = clr[t].transpose(0, 2, 1).reshape(q, 2, p, c)
        i = -cli[t].transpose(0, 2, 1).reshape(q, 2, p, c)
        return cat([pair_diag(r[:, 0], r[:, 1]), pair_diag(i[:, 0], i[:, 1])], axis=1)
    mc = cat([mc_cols(t) for t in range(L)], axis=2)
    sr, si = lam_pow(L * (2 ** jnp.arange(nlev)))
    lp = jnp.stack([sr.reshape(nlev, SSM_PAIRS, 2 * p), si.reshape(nlev, SSM_PAIRS, 2 * p)], axis=2)
    lp = lp.reshape(nlev, SSM_ROW)
    lp = jnp.pad(lp, ((0, (-nlev) % SUBLANES), (0, 0)))
    dd = jnp.broadcast_to(d_skip.astype(F32).reshape(SSM_PAIRS, 1, SSM_PAIR_CH),
                          (SSM_PAIRS, L, SSM_PAIR_CH)).reshape(1, SSM_ROW)
    return mb.astype(BF16), tt.astype(BF16), mc.astype(BF16), lp, dd


def _head_pad_cols(nope, rope):
    k = nope.shape[0]
    parts = [nope]
    if rope is not None:
        parts.append(rope)
    used = sum(t.shape[-1] for t in parts)
    parts.append(jnp.zeros((k, N_HEADS, HEAD_PAD - used), nope.dtype))
    return jnp.concatenate(parts, axis=-1).reshape(k, QK_WIDTH)


def _rope_tables(s):
    inv = ROPE_BASE ** (-jnp.arange(ROPE_HALF, dtype=F32) / ROPE_HALF)
    ang = jnp.arange(s, dtype=F32)[:, None] * inv[None, :]
    cos, sin = jnp.cos(ang), jnp.sin(ang)
    gap = jnp.zeros((s, QK_NOPE_DIM - QK_ROPE_DIM), F32)
    tail = jnp.ones((s, HEAD_PAD - QK_NOPE_DIM - QK_ROPE_DIM), F32)
    return jnp.concatenate([sin, -sin, gap, cos, cos, tail], axis=1)


def _pick(total, want):
    t = min(total, want)
    assert total % t == 0
    return t


def kernel(x, l0_mix_norm, l0_w_in, l0_conv_w, l0_conv_b, l0_conv_ln_g, l0_conv_ln_b, l0_q_norm, l0_kv_norm, l0_w_uq, l0_w_ukv, l0_w_out, l0_ffn_norm, l0_w_up, l0_ffn_conv_w, l0_ffn_conv_b, l0_w_down, l1_mix_norm, l1_w_in, l1_log_dt, l1_a_re, l1_a_im, l1_b_re, l1_b_im, l1_c_re, l1_c_im, l1_d, l1_w_glu, l1_b_glu, l1_ffn_norm, l1_w_up, l1_ffn_conv_w, l1_ffn_conv_b, l1_w_down, final_norm):
    b, s, d = x.shape
    row2 = lambda v: v.reshape(1, -1).astype(F32)

    o = 2 * CONV_WIDTH + Q_LORA_RANK + KV_LORA_RANK
    w_kr = l0_w_in[:, o:]
    kr_blk = jnp.concatenate([jnp.zeros((d, QK_NOPE_DIM), F32), w_kr,
                              jnp.zeros((d, HEAD_PAD - QK_NOPE_DIM - QK_ROPE_DIM), F32)], axis=1)
    w_in0 = jnp.concatenate([l0_w_in[:, :o], kr_blk], axis=1).astype(BF16)
    scale = (QK_NOPE_DIM + QK_ROPE_DIM) ** -0.5 * math.log2(math.e)
    wq = (l0_w_uq * scale).reshape(Q_LORA_RANK, N_HEADS, QK_NOPE_DIM + QK_ROPE_DIM)
    wuq = _head_pad_cols(wq[..., :QK_NOPE_DIM], wq[..., QK_NOPE_DIM:]).astype(BF16)
    wkv = l0_w_ukv.reshape(KV_LORA_RANK, N_HEADS, QK_NOPE_DIM + V_HEAD_DIM)
    wukv = jnp.concatenate([_head_pad_cols(wkv[..., :QK_NOPE_DIM], None),
                            wkv[..., QK_NOPE_DIM:].reshape(KV_LORA_RANK, ATTN_WIDTH)], axis=1).astype(BF16)
    cw0 = jnp.pad(l0_conv_w.astype(F32), ((0, CONV_HALO - CONV_K), (0, 0)))
    rope = _rope_tables(s)

    t_attn = _pick(s, 512)
    u0, q, k, vt = _pre0(x, row2(l0_mix_norm), w_in0, rope, cw0, row2(l0_conv_b),
                         row2(l0_conv_ln_g), row2(l0_conv_ln_b), row2(l0_q_norm), row2(l0_kv_norm),
                         wuq, wukv, tm=t_attn)
    attn = _attention(q, k, vt, t=t_attn)
    w_out = l0_w_out.astype(BF16)
    mix0 = (u0, attn, w_out[:CONV_WIDTH], w_out[CONV_WIDTH:])

    def ffn(xx, mix, g, w_up, cw, cb, w_down, final):
        taps = jnp.concatenate([cw.astype(F32), cb.astype(F32)[None],
                                jnp.zeros((SUBLANES - FFN_K - 1, 2 * D_FF), F32)], axis=0)
        return _ffn(xx, mix, row2(g), w_up.astype(BF16), taps, w_down.astype(BF16),
                    row2(final_norm), tm=_pick(s, 512), final_norm=final)

    x2 = ffn(x, mix0, l0_ffn_norm, l0_w_up, l0_ffn_conv_w, l0_ffn_conv_b, l0_w_down, False)

    rows = _pick(s // SSM_L, 128)
    nlev = int(math.log2(rows))
    mb, tt, mc, lp, dd = _ssm_tables(l1_log_dt, l1_a_re, l1_a_im, l1_b_re, l1_b_im,
                                     l1_c_re, l1_c_im, l1_d, nlev)
    x3 = _mix1(x2, row2(l1_mix_norm), l1_w_in.astype(BF16), mb, tt, mc, lp, dd,
               l1_w_glu.astype(BF16), row2(l1_b_glu), rows=rows)
    return ffn(x3, None, l1_ffn_norm, l1_w_up, l1_ffn_conv_w, l1_ffn_conv_b, l1_w_down, True)
```

```python
import functools
import math

import jax
import jax.numpy as jnp
from jax import lax
from jax.experimental import pallas as pl
from jax.experimental.pallas import tpu as pltpu

F32 = jnp.float32
BF16 = jnp.bfloat16

D_MODEL = 1024
EPS = 1e-6
LN_EPS = 1e-5
CONV_WIDTH = 512
CONV_K = 31
N_HEADS = 8
QK_NOPE_DIM = 64
QK_ROPE_DIM = 32
ROPE_HALF = QK_ROPE_DIM // 2
V_HEAD_DIM = 64
Q_LORA_RANK = 256
KV_LORA_RANK = 128
ROPE_BASE = 10000.0
ATTN_WIDTH = N_HEADS * V_HEAD_DIM
SSM_WIDTH = 512
SSM_GROUP = 16
SSM_GROUPS = SSM_WIDTH // SSM_GROUP
SSM_STATE = 64
D_FF = 2816
FFN_K = 3

LANES = 128
SUBLANES = 8
V_ROWS = V_HEAD_DIM + 16
HEAD_PAD = LANES
QK_WIDTH = N_HEADS * HEAD_PAD
CONV_HALO = 32
CONV_ROWS = 64
FFN_HALO = SUBLANES
FFN_CHUNK = 256
FFN_NCHUNK = D_FF // FFN_CHUNK
SSM_L = 8
SSM_PAIRS = SSM_GROUPS // 2
SSM_PAIR_CH = 2 * SSM_GROUP
SSM_TILE = SSM_L * SSM_PAIR_CH
SSM_ROW = SSM_L * SSM_WIDTH
VMEM_LIMIT = 56 * 1024 * 1024


def _cparams(sem):
    return pltpu.CompilerParams(dimension_semantics=sem, vmem_limit_bytes=VMEM_LIMIT)


def _rms(x, g):
    return x * lax.rsqrt(jnp.mean(x * x, axis=-1, keepdims=True) + EPS) * g


def _sigmoid(x):
    return 1.0 / (1.0 + jnp.exp(-x))


def _rope128(t, cos, sin_up, sin_dn):
    return (t * cos + pltpu.roll(t, ROPE_HALF, axis=1) * sin_up
            + pltpu.roll(t, LANES - ROPE_HALF, axis=1) * sin_dn)


def _rope_unpack(packed):
    lane = lax.broadcasted_iota(jnp.int32, packed.shape, 1)
    r1, r2 = QK_NOPE_DIM, QK_NOPE_DIM + ROPE_HALF
    cos = jnp.where(lane < r1, 1.0, packed)
    up = pltpu.roll(packed, r2, axis=1)
    dn = pltpu.roll(packed, r1 - ROPE_HALF, axis=1)
    sin_up = jnp.where(lane >= r2, jnp.where(lane < r2 + ROPE_HALF, up, 0.0), 0.0)
    sin_dn = jnp.where(lane >= r1, jnp.where(lane < r2, dn, 0.0), 0.0)
    return cos, sin_up, sin_dn


def _pre0_kernel(x_ref, g_ref, w_in_ref, rope_ref, cw_ref, cb_ref, lng_ref, lnb_ref,
                 qn_ref, kvn_ref, wuq_ref, wukv_ref,
                 u_ref, q_ref, k_ref, vt_ref, ubuf, ushift, h_scr, q_scr, kv_scr, *, tm):
    sb = pl.program_id(1)

    @pl.when(sb == 0)
    def _():
        ubuf[0:CONV_HALO, :] = jnp.zeros((CONV_HALO, CONV_WIDTH), F32)

    hm = tm // 2
    n_in = w_in_ref.shape[1] // (2 * LANES)
    base = CONV_HALO - (CONV_K - 1)

    def in_proj(half):
        xn = _rms(x_ref[0, half * hm:(half + 1) * hm, :], g_ref[...]).astype(BF16)

        def piece(n):
            cols = slice(n * 2 * LANES, (n + 1) * 2 * LANES)
            h_scr[half, :, cols] = jnp.dot(xn, w_in_ref[:, cols], preferred_element_type=F32)
        return [functools.partial(piece, n) for n in range(n_in)]

    def qkv_proj(half):
        h = h_scr.at[half]
        o = 2 * CONV_WIDTH

        def q_piece():
            cq = _rms(h[:, o:o + Q_LORA_RANK], qn_ref[...]).astype(BF16)
            q_scr[...] = jnp.dot(cq, wuq_ref[...], preferred_element_type=F32)

        def kv_piece():
            o2 = o + Q_LORA_RANK
            ckv = _rms(h[:, o2:o2 + KV_LORA_RANK], kvn_ref[...]).astype(BF16)
            kv_scr[...] = jnp.dot(ckv, wukv_ref[...], preferred_element_type=F32)
        return [q_piece, kv_piece]

    def conv(half, between):
        h = h_scr.at[half]
        t0 = half * hm
        ubuf[CONV_HALO + t0:CONV_HALO + t0 + hm, :] = h[:, :CONV_WIDTH] * _sigmoid(h[:, CONV_WIDTH:2 * CONV_WIDTH])
        span = hm + CONV_HALO - SUBLANES
        for r in range(1, SUBLANES):
            ushift[r - 1, t0:t0 + span, :] = ubuf[t0 + r:t0 + r + span, :]
        chunks = range(t0, t0 + hm, CONV_ROWS)
        for i, r0 in enumerate(chunks):
            acc = jnp.broadcast_to(cb_ref[...], (CONV_ROWS, CONV_WIDTH))
            for o in range(base, base + CONV_K):
                r, al = o % SUBLANES, o - o % SUBLANES + r0
                tap = ushift[r - 1, al:al + CONV_ROWS, :] if r else ubuf[al:al + CONV_ROWS, :]
                acc = acc + cw_ref[o - base:o - base + 1, :] * tap
            mu = jnp.mean(acc, axis=-1, keepdims=True)
            cen = acc - mu
            var = jnp.mean(cen * cen, axis=-1, keepdims=True)
            y = cen * lax.rsqrt(var + LN_EPS) * lng_ref[...] + lnb_ref[...]
            u_ref[0, r0:r0 + CONV_ROWS, :] = (y * _sigmoid(y)).astype(BF16)
            for thunk in between[i * len(between) // len(chunks):(i + 1) * len(between) // len(chunks)]:
                thunk()

    def rotary_out(half):
        rows = slice(half * hm, (half + 1) * hm)
        cos, sin_up, sin_dn = _rope_unpack(rope_ref[rows, :])
        o = 2 * CONV_WIDTH + Q_LORA_RANK + KV_LORA_RANK
        kr = _rope128(h_scr[half, :, o:o + LANES], cos, sin_up, sin_dn)
        for hd in range(N_HEADS):
            sl = slice(hd * HEAD_PAD, (hd + 1) * HEAD_PAD)
            q_ref[0, rows, sl] = _rope128(q_scr[:, sl], cos, sin_up, sin_dn).astype(BF16)
            k_ref[0, rows, sl] = (kv_scr[:, sl] + kr).astype(BF16)
        vt = kv_scr[:, QK_WIDTH:].T
        ones = jnp.ones((V_ROWS - V_HEAD_DIM, hm), F32)
        parts = [blk for hd in range(N_HEADS) for blk in (vt[hd * V_HEAD_DIM:(hd + 1) * V_HEAD_DIM], ones)]
        vt_ref[0, 0, :, rows] = jnp.concatenate(parts, axis=0).astype(BF16)

    for thunk in in_proj(0):
        thunk()
    conv(0, in_proj(1))
    conv(1, qkv_proj(0))
    rotary_out(0)
    for thunk in qkv_proj(1):
        thunk()
    rotary_out(1)
    ubuf[0:CONV_HALO, :] = ubuf[tm:tm + CONV_HALO, :]


def _pre0(x, g, w_in, rope, cw, cb, lng, lnb, qn, kvn, wuq, wukv, *, tm):
    b, s, d = x.shape
    const = lambda shape: pl.BlockSpec(shape, lambda i, j: (0,) * len(shape))
    row = lambda w: pl.BlockSpec((1, tm, w), lambda i, j: (i, j, 0))
    return pl.pallas_call(
        functools.partial(_pre0_kernel, tm=tm),
        grid=(b, s // tm),
        in_specs=[row(d), const(g.shape), const(w_in.shape),
                  pl.BlockSpec((tm, LANES), lambda i, j: (j, 0)),
                  const(cw.shape), const(cb.shape), const(lng.shape), const(lnb.shape),
                  const(qn.shape), const(kvn.shape), const(wuq.shape), const(wukv.shape)],
        out_specs=[row(CONV_WIDTH), row(QK_WIDTH), row(QK_WIDTH),
                   pl.BlockSpec((1, 1, N_HEADS * V_ROWS, tm), lambda i, j: (i, j, 0, 0))],
        out_shape=[jax.ShapeDtypeStruct((b, s, CONV_WIDTH), BF16),
                   jax.ShapeDtypeStruct((b, s, QK_WIDTH), BF16),
                   jax.ShapeDtypeStruct((b, s, QK_WIDTH), BF16),
                   jax.ShapeDtypeStruct((b, s // tm, N_HEADS * V_ROWS, tm), BF16)],
        scratch_shapes=[pltpu.VMEM((tm + CONV_HALO, CONV_WIDTH), F32),
                        pltpu.VMEM((SUBLANES - 1, tm + CONV_HALO, CONV_WIDTH), F32),
                        pltpu.VMEM((2, tm // 2, w_in.shape[1]), F32),
                        pltpu.VMEM((tm // 2, wuq.shape[1]), F32),
                        pltpu.VMEM((tm // 2, wukv.shape[1]), F32)],
        compiler_params=_cparams(("arbitrary", "arbitrary")),
        name="pre0",
    )(x, g, w_in, rope, cw, cb, lng, lnb, qn, kvn, wuq, wukv)


NEG_BIG = -1e30
KV_UNROLL = 4


def _attn_kernel(q_ref, k_ref, vt_ref, o_ref, s0_ref, s1_ref, *, t):
    qi = pl.program_id(2)
    s_refs = (s0_ref, s1_ref)

    def scores(j, hh):
        lanes = slice(hh * HEAD_PAD, (hh + 1) * HEAD_PAD)
        k = k_ref[0, pl.ds(pl.multiple_of(j * t, t), t), lanes]
        s_refs[hh][...] = lax.dot_general(k, q_ref[0, :, lanes], (((1,), (1,)), ((), ())),
                                          preferred_element_type=F32)

    def update(j, hh, state, masked):
        m, acc = state
        st = s_refs[hh][...]
        if masked:
            key = lax.broadcasted_iota(jnp.int32, (t, t), 0)
            qry = lax.broadcasted_iota(jnp.int32, (t, t), 1)
            st = jnp.where(key <= qry, st, NEG_BIG)
        m_new = jnp.maximum(m, jnp.max(st, axis=0, keepdims=True))
        alpha = jnp.exp2(m - m_new)
        p = jnp.exp2(st - m_new).astype(BF16)
        vt = vt_ref[0, j, hh * V_ROWS:(hh + 1) * V_ROWS, :]
        acc = alpha * acc + jnp.dot(vt, p, preferred_element_type=F32)
        return m_new, acc

    def body(j, carry):
        scores(j, 1)
        st0 = update(j, 0, carry[:2], False)
        scores(j + 1, 0)
        st1 = update(j, 1, carry[2:], False)
        return st0 + st1

    init = (jnp.full((1, t), NEG_BIG, F32), jnp.zeros((V_ROWS, t), F32)) * 2
    scores(0, 0)
    def unrolled(i, c):
        for u in range(KV_UNROLL):
            c = body(KV_UNROLL * i + u, c)
        return c

    carry = lax.fori_loop(0, qi // KV_UNROLL, unrolled, init)
    carry = lax.fori_loop(qi // KV_UNROLL * KV_UNROLL, qi, body, carry)
    scores(qi, 1)
    _, acc0 = update(qi, 0, carry[:2], True)
    _, acc1 = update(qi, 1, carry[2:], True)
    d = V_HEAD_DIM
    ot = jnp.concatenate([acc0[:d] / acc0[d:d + 1], acc1[:d] / acc1[d:d + 1]], axis=0)
    o_ref[0] = ot.T.astype(o_ref.dtype)


def _attention(q, k, vt, *, t):
    b, s, _ = q.shape
    assert vt.shape == (b, s // t, N_HEADS * V_ROWS, t)
    return pl.pallas_call(
        functools.partial(_attn_kernel, t=t),
        grid=(b, N_HEADS // 2, s // t),
        in_specs=[pl.BlockSpec((1, t, 2 * HEAD_PAD), lambda i, h, j: (i, j, h)),
                  pl.BlockSpec((1, s, 2 * HEAD_PAD), lambda i, h, j: (i, 0, h)),
                  pl.BlockSpec((1, s // t, 2 * V_ROWS, t), lambda i, h, j: (i, 0, h, 0))],
        out_specs=pl.BlockSpec((1, t, 2 * V_HEAD_DIM), lambda i, h, j: (i, j, h)),
        out_shape=jax.ShapeDtypeStruct((b, s, ATTN_WIDTH), BF16),
        scratch_shapes=[pltpu.VMEM((t, t), F32), pltpu.VMEM((t, t), F32)],
        compiler_params=_cparams(("arbitrary", "arbitrary", "arbitrary")),
        name="attn",
    )(q, k, vt)


def _ffn_kernel(*refs, tm, mix_out, final_norm):
    if mix_out:
        x_ref, u_ref, a_ref, wu_ref, wa_ref, *refs = refs
    else:
        x_ref, *refs = refs
    g_ref, wup_ref, cw_ref, wd_ref, fg_ref, o_ref, xn_scr, hbuf0, hbuf1, abuf0, abuf1, carry = refs
    sb = pl.program_id(1)
    nsub = x_ref.shape[1] // tm
    halves = lambda c: ((slice(0, FFN_CHUNK), c * FFN_CHUNK),
                        (slice(FFN_CHUNK, 2 * FFN_CHUNK), D_FF + c * FFN_CHUNK))

    def start(r):
        rows = slice(r * tm, (r + 1) * tm)
        x = x_ref[0, rows, :]
        if mix_out:
            x = (x + jnp.dot(u_ref[0, rows, :], wu_ref[...], preferred_element_type=F32)
                 + jnp.dot(a_ref[0, rows, :], wa_ref[...], preferred_element_type=F32))
        xn_scr[r] = _rms(x, g_ref[...]).astype(BF16)
        o_ref[0, rows, :] = x

    def up(r, c, hbuf):
        first = sb == 0 if r == 0 else False
        for lanes, col in halves(c):
            h = jnp.dot(xn_scr[r], wup_ref[:, col:col + FFN_CHUNK], preferred_element_type=F32)
            hbuf[0:FFN_HALO, lanes] = jnp.where(first, 0.0, carry[c, :, lanes])
            hbuf[FFN_HALO:FFN_HALO + tm, lanes] = h
            carry[c, :, lanes] = h[tm - FFN_HALO:, :]

    def gate(c, hbuf, abuf):
        cv = []
        for lanes, col in halves(c):
            w = cw_ref[:, col:col + FFN_CHUNK]
            v = w[FFN_K - 1:FFN_K, :] * hbuf[FFN_HALO:FFN_HALO + tm, lanes] + w[FFN_K:FFN_K + 1, :]
            for k in range(FFN_K - 1):
                off = FFN_HALO - (FFN_K - 1) + k
                v = v + w[k:k + 1, :] * hbuf[off:off + tm, lanes]
            cv.append(v)
        abuf[...] = (cv[0] * _sigmoid(cv[0]) * cv[1]).astype(BF16)

    def down(r, c, abuf):
        rows = slice(r * tm, (r + 1) * tm)
        o_ref[0, rows, :] += jnp.dot(abuf[...], wd_ref[c * FFN_CHUNK:(c + 1) * FFN_CHUNK, :],
                                     preferred_element_type=F32)
        if final_norm and c == FFN_NCHUNK - 1:
            o_ref[0, rows, :] = _rms(o_ref[0, rows, :], fg_ref[...])

    hb, ab = (hbuf0, hbuf1), (abuf0, abuf1)
    items = [(r, c) for r in range(nsub) for c in range(FFN_NCHUNK)]
    start(0)
    up(0, 0, hb[0])
    for i, (r, c) in enumerate(items):
        if i + 1 < len(items):
            rn, cn = items[i + 1]
            if cn == 0:
                start(rn)
            up(rn, cn, hb[(i + 1) % 2])
        if i:
            down(*items[i - 1], ab[(i - 1) % 2])
        gate(c, hb[i % 2], ab[i % 2])
    down(*items[-1], ab[(len(items) - 1) % 2])


def _ffn(x, mix, g, wup, cw, wd, fg, *, tm, nsub, final_norm):
    b, s, d = x.shape
    const = lambda a: pl.BlockSpec(a.shape, lambda i, j: (0,) * a.ndim, pipeline_mode=pl.Buffered(1))
    row = lambda w: pl.BlockSpec((1, nsub * tm, w), lambda i, j: (i, j, 0))
    ins, specs = [x], [row(d)]
    if mix is not None:
        u, a, wu, wa = mix
        ins += [u, a, wu, wa]
        specs += [row(u.shape[-1]), row(a.shape[-1]), const(wu), const(wa)]
    ins += [g, wup, cw, wd, fg]
    specs += [const(g), const(wup), const(cw), const(wd), const(fg)]
    return pl.pallas_call(
        functools.partial(_ffn_kernel, tm=tm, mix_out=mix is not None, final_norm=final_norm),
        grid=(b, s // (nsub * tm)),
        in_specs=specs,
        out_specs=row(d),
        out_shape=jax.ShapeDtypeStruct((b, s, d), F32),
        scratch_shapes=[pltpu.VMEM((nsub, tm, d), BF16),
                        pltpu.VMEM((tm + FFN_HALO, 2 * FFN_CHUNK), F32),
                        pltpu.VMEM((tm + FFN_HALO, 2 * FFN_CHUNK), F32),
                        pltpu.VMEM((tm, FFN_CHUNK), BF16),
                        pltpu.VMEM((tm, FFN_CHUNK), BF16),
                        pltpu.VMEM((FFN_NCHUNK, FFN_HALO, 2 * FFN_CHUNK), F32)],
        compiler_params=_cparams(("arbitrary", "arbitrary")),
        name="ffn_final" if final_norm else "ffn",
    )(*ins)


def _regroup(slabs, n_out, pieces):
    rows = slabs[0].shape[0]
    quarter = lax.broadcasted_iota(jnp.int32, (rows, LANES), 1) // SSM_PAIR_CH
    out = []
    for d in range(n_out):
        acc = None
        for dst_q, (src, src_q) in enumerate(pieces(d)):
            shift = ((dst_q - src_q) * SSM_PAIR_CH) % LANES
            piece = pltpu.roll(slabs[src], shift, axis=1) if shift else slabs[src]
            acc = piece if acc is None else jnp.where(quarter == dst_q, piece, acc)
        out.append(acc)
    return out


def _shift_rows(x, n, row):
    return jnp.where(row >= n, pltpu.roll(x, n, axis=0), 0.0)


def _mix1_kernel(x_ref, g_ref, win_ref, mb_ref, t_ref, mc_ref, lp_ref, d_ref, wglu_ref, bglu_ref,
                 o_ref, carry, u_slab, y_slab, *, rows, nlev):
    sb = pl.program_id(1)

    @pl.when(sb == 0)
    def _():
        carry[...] = jnp.zeros(carry.shape, F32)

    x = x_ref[0]
    xn = _rms(x, g_ref[...]).astype(BF16)
    u = jnp.dot(xn, win_ref[...], preferred_element_type=F32)
    nch = SSM_WIDTH // LANES
    for s in range(nch):
        u_slab[s] = u[:, s * LANES:(s + 1) * LANES]
    nslab = SSM_ROW // LANES
    u_slabs = [u_slab[i % nch, pl.ds(i // nch, rows, stride=SSM_L), :] for i in range(nslab)]
    a_slabs = _regroup(u_slabs, nslab,
                       lambda d: [(4 * (4 * (d % 2) + i) + (d // 2) // 4, (d // 2) % 4) for i in range(4)])
    row = lax.broadcasted_iota(jnp.int32, (rows, LANES), 0)
    first = row == 0
    y_slabs = []
    for p in range(SSM_PAIRS):
        a = jnp.concatenate(a_slabs[2 * p:2 * p + 2], axis=1)
        a16 = a.astype(BF16)
        lo = slice(p * SSM_TILE, p * SSM_TILE + LANES)
        hi = slice(p * SSM_TILE + LANES, (p + 1) * SSM_TILE)
        sloc = jnp.dot(a16, mb_ref[p], preferred_element_type=F32)
        sr, si = sloc[:, :LANES], sloc[:, LANES:]
        cr, ci = carry[0:1, lo], carry[0:1, hi]
        lr, li = lp_ref[0:1, lo], lp_ref[0:1, hi]
        sr = sr + jnp.where(first, lr * cr - li * ci, 0.0)
        si = si + jnp.where(first, lr * ci + li * cr, 0.0)
        for lev in range(nlev):
            n = 1 << lev
            lr, li = lp_ref[lev:lev + 1, lo], lp_ref[lev:lev + 1, hi]
            pr, pi = _shift_rows(sr, n, row), _shift_rows(si, n, row)
            sr, si = sr + (lr * pr - li * pi), si + (lr * pi + li * pr)
        xr = jnp.where(first, cr, pltpu.roll(sr, 1, axis=0))
        xi = jnp.where(first, ci, pltpu.roll(si, 1, axis=0))
        carry[0:1, lo] = sr[rows - 1:rows, :]
        carry[0:1, hi] = si[rows - 1:rows, :]
        xp = jnp.concatenate([xr, xi], axis=1).astype(BF16)
        y = (jnp.dot(a16, t_ref[p], preferred_element_type=F32)
             + jnp.dot(xp, mc_ref[p], preferred_element_type=F32)
             + d_ref[:, p * SSM_TILE:(p + 1) * SSM_TILE] * a)
        y = jax.nn.gelu(y, approximate=True)
        y_slabs += [y[:, :LANES], y[:, LANES:]]
    o_slabs = _regroup(y_slabs, nslab,
                       lambda s: [(2 * (4 * (s % 4) + j) + (s // 4) // 4, (s // 4) % 4) for j in range(4)])
    for i in range(nslab):
        y_slab[i % nch, pl.ds(i // nch, rows, stride=SSM_L), :] = o_slabs[i]
    y = jnp.concatenate([y_slab[s] for s in range(nch)], axis=1).astype(BF16)
    z = jnp.dot(y, wglu_ref[...], preferred_element_type=F32) + bglu_ref[...]
    o_ref[0] = x + z[:, :D_MODEL] * _sigmoid(z[:, D_MODEL:])


def _mix1(x, g, win, mb, tt, mc, lp, dd, wglu, bglu, *, rows):
    b, s, d = x.shape
    tok = rows * SSM_L
    nlev = int(math.log2(rows))
    assert 1 << nlev == rows and lp.shape[0] >= nlev
    const = lambda a: pl.BlockSpec(a.shape, lambda i, j: (0,) * a.ndim, pipeline_mode=pl.Buffered(1))
    return pl.pallas_call(
        functools.partial(_mix1_kernel, rows=rows, nlev=nlev),
        grid=(b, s // tok),
        in_specs=[pl.BlockSpec((1, tok, d), lambda i, j: (i, j, 0)),
                  const(g), const(win), const(mb), const(tt), const(mc), const(lp), const(dd),
                  const(wglu), const(bglu)],
        out_specs=pl.BlockSpec((1, tok, d), lambda i, j: (i, j, 0)),
        out_shape=jax.ShapeDtypeStruct((b, s, d), F32),
        scratch_shapes=[pltpu.VMEM((SUBLANES, SSM_ROW), F32),
                        pltpu.VMEM((SSM_WIDTH // LANES, tok, LANES), F32),
                        pltpu.VMEM((SSM_WIDTH // LANES, tok, LANES), F32)],
        compiler_params=_cparams(("arbitrary", "arbitrary")),
        name="mix1",
    )(x, g, win, mb, tt, mc, lp, dd, wglu, bglu)


def _ssm_tables(log_dt, a_re, a_im, b_re, b_im, c_re, c_im, d_skip, nlev):
    g, p, c, L = SSM_GROUPS, SSM_STATE, SSM_GROUP, SSM_L
    dt = jnp.exp(log_dt.astype(F32))[:, None]
    ar, ai = a_re.astype(F32), a_im.astype(F32)
    mag = jnp.exp(ar * dt)
    lb_re, lb_im = mag * jnp.cos(ai * dt), mag * jnp.sin(ai * dt)
    den = ar * ar + ai * ai
    nr, ni = lb_re - 1.0, lb_im
    f_re = (nr * ar + ni * ai) / den
    f_im = (ni * ar - nr * ai) / den
    br, bi = b_re.astype(F32), b_im.astype(F32)
    bb_re = f_re[..., None] * br - f_im[..., None] * bi
    bb_im = f_re[..., None] * bi + f_im[..., None] * br

    def lam_pow(n):
        n = jnp.asarray(n, F32)[:, None, None]
        m = jnp.exp(n * (ar * dt))
        return m * jnp.cos(n * (ai * dt)), m * jnp.sin(n * (ai * dt))

    cr, ci = c_re.astype(F32), c_im.astype(F32)
    pr, pi = lam_pow(jnp.arange(L + 1))
    lbr = pr[..., None] * bb_re[None] - pi[..., None] * bb_im[None]
    lbi = pr[..., None] * bb_im[None] + pi[..., None] * bb_re[None]
    kk = (jnp.einsum('gop,ngpc->ngoc', cr, lbr[:L]) - jnp.einsum('gop,ngpc->ngoc', ci, lbi[:L]))
    cat = jnp.concatenate
    q = SSM_PAIRS

    def pair_diag(a0, a1):
        z = jnp.zeros_like(a0)
        return cat([cat([a0, z], axis=2), cat([z, a1], axis=2)], axis=1)

    kt = kk.transpose(0, 1, 3, 2).reshape(L, q, 2, c, c)
    kbd = [pair_diag(kt[n, :, 0], kt[n, :, 1]) for n in range(L)]
    zero = jnp.zeros_like(kbd[0])
    tt = cat([cat([kbd[t2 - t1] if t2 >= t1 else zero for t2 in range(L)], axis=2) for t1 in range(L)], axis=1)
    def mb_rows(n):
        r = lbr[n].transpose(0, 2, 1).reshape(q, 2, c, p)
        i = lbi[n].transpose(0, 2, 1).reshape(q, 2, c, p)
        return cat([pair_diag(r[:, 0], r[:, 1]), pair_diag(i[:, 0], i[:, 1])], axis=2)
    mb = cat([mb_rows(L - 1 - t) for t in range(L)], axis=1)
    p1r, p1i = pr[1:], pi[1:]
    clr = cr[None] * p1r[:, :, None, :] - ci[None] * p1i[:, :, None, :]
    cli = cr[None] * p1i[:, :, None, :] + ci[None] * p1r[:, :, None, :]
    def mc_cols(t):
        r = clr[t].transpose(0, 2, 1).reshape(q, 2, p, c)
        i = -cli[t].transpose(0, 2, 1).reshape(q, 2, p, c)
        return cat([pair_diag(r[:, 0], r[:, 1]), pair_diag(i[:, 0], i[:, 1])], axis=1)
    mc = cat([mc_cols(t) for t in range(L)], axis=2)
    sr, si = lam_pow(L * (2 ** jnp.arange(nlev)))
    lp = jnp.stack([sr.reshape(nlev, SSM_PAIRS, 2 * p), si.reshape(nlev, SSM_PAIRS, 2 * p)], axis=2)
    lp = lp.reshape(nlev, SSM_ROW)
    lp = jnp.pad(lp, ((0, (-nlev) % SUBLANES), (0, 0)))
    dd = jnp.broadcast_to(d_skip.astype(F32).reshape(SSM_PAIRS, 1, SSM_PAIR_CH),
                          (SSM_PAIRS, L, SSM_PAIR_CH)).reshape(1, SSM_ROW)
    return mb.astype(BF16), tt.astype(BF16), mc.astype(BF16), lp, dd


def _head_pad_cols(nope, rope):
    k = nope.shape[0]
    parts = [nope]
    if rope is not None:
        parts.append(rope)
    used = sum(t.shape[-1] for t in parts)
    parts.append(jnp.zeros((k, N_HEADS, HEAD_PAD - used), nope.dtype))
    return jnp.concatenate(parts, axis=-1).reshape(k, QK_WIDTH)


def _rope_tables(s):
    inv = ROPE_BASE ** (-jnp.arange(ROPE_HALF, dtype=F32) / ROPE_HALF)
    ang = jnp.arange(s, dtype=F32)[:, None] * inv[None, :]
    cos, sin = jnp.cos(ang), jnp.sin(ang)
    gap = jnp.zeros((s, QK_NOPE_DIM - QK_ROPE_DIM), F32)
    tail = jnp.ones((s, HEAD_PAD - QK_NOPE_DIM - QK_ROPE_DIM), F32)
    return jnp.concatenate([sin, -sin, gap, cos, cos, tail], axis=1)


def _pick(total, want):
    t = min(total, want)
    assert total % t == 0
    return t


def kernel(x, l0_mix_norm, l0_w_in, l0_conv_w, l0_conv_b, l0_conv_ln_g, l0_conv_ln_b, l0_q_norm, l0_kv_norm, l0_w_uq, l0_w_ukv, l0_w_out, l0_ffn_norm, l0_w_up, l0_ffn_conv_w, l0_ffn_conv_b, l0_w_down, l1_mix_norm, l1_w_in, l1_log_dt, l1_a_re, l1_a_im, l1_b_re, l1_b_im, l1_c_re, l1_c_im, l1_d, l1_w_glu, l1_b_glu, l1_ffn_norm, l1_w_up, l1_ffn_conv_w, l1_ffn_conv_b, l1_w_down, final_norm):
    b, s, d = x.shape
    row2 = lambda v: v.reshape(1, -1).astype(F32)

    o = 2 * CONV_WIDTH + Q_LORA_RANK + KV_LORA_RANK
    w_kr = l0_w_in[:, o:]
    kr_blk = jnp.concatenate([jnp.zeros((d, QK_NOPE_DIM), F32), w_kr,
                              jnp.zeros((d, HEAD_PAD - QK_NOPE_DIM - QK_ROPE_DIM), F32)], axis=1)
    w_in0 = jnp.concatenate([l0_w_in[:, :o], kr_blk], axis=1).astype(BF16)
    scale = (QK_NOPE_DIM + QK_ROPE_DIM) ** -0.5 * math.log2(math.e)
    wq = (l0_w_uq * scale).reshape(Q_LORA_RANK, N_HEADS, QK_NOPE_DIM + QK_ROPE_DIM)
    wuq = _head_pad_cols(wq[..., :QK_NOPE_DIM], wq[..., QK_NOPE_DIM:]).astype(BF16)
    wkv = l0_w_ukv.reshape(KV_LORA_RANK, N_HEADS, QK_NOPE_DIM + V_HEAD_DIM)
    wukv = jnp.concatenate([_head_pad_cols(wkv[..., :QK_NOPE_DIM], None),
                            wkv[..., QK_NOPE_DIM:].reshape(KV_LORA_RANK, ATTN_WIDTH)], axis=1).astype(BF16)
    cw0 = jnp.pad(l0_conv_w.astype(F32), ((0, CONV_HALO - CONV_K), (0, 0)))
    rope = _rope_tables(s)

    t_attn = _pick(s, 512)
    u0, q, k, vt = _pre0(x, row2(l0_mix_norm), w_in0, rope, cw0, row2(l0_conv_b),
                         row2(l0_conv_ln_g), row2(l0_conv_ln_b), row2(l0_q_norm), row2(l0_kv_norm),
                         wuq, wukv, tm=t_attn)
    attn = _attention(q, k, vt, t=t_attn)
    w_out = l0_w_out.astype(BF16)
    mix0 = (u0, attn, w_out[:CONV_WIDTH], w_out[CONV_WIDTH:])

    tm_ffn = _pick(s, 512)

    def ffn(xx, mix, g, w_up, cw, cb, w_down, final):
        taps = jnp.concatenate([cw.astype(F32), cb.astype(F32)[None],
                                jnp.zeros((SUBLANES - FFN_K - 1, 2 * D_FF), F32)], axis=0)
        return _ffn(xx, mix, row2(g), w_up.astype(BF16), taps, w_down.astype(BF16),
                    row2(final_norm), tm=tm_ffn, nsub=1, final_norm=final)

    x2 = ffn(x, mix0, l0_ffn_norm, l0_w_up, l0_ffn_conv_w, l0_ffn_conv_b, l0_w_down, False)

    rows = _pick(s // SSM_L, 128)
    nlev = int(math.log2(rows))
    mb, tt, mc, lp, dd = _ssm_tables(l1_log_dt, l1_a_re, l1_a_im, l1_b_re, l1_b_im,
                                     l1_c_re, l1_c_im, l1_d, nlev)
    x3 = _mix1(x2, row2(l1_mix_norm), l1_w_in.astype(BF16), mb, tt, mc, lp, dd,
               l1_w_glu.astype(BF16), row2(l1_b_glu), rows=rows)
    return ffn(x3, None, l1_ffn_norm, l1_w_up, l1_ffn_conv_w, l1_ffn_conv_b, l1_w_down, True)
```

```python
import functools
import math

import jax
import jax.numpy as jnp
from jax import lax
from jax.experimental import pallas as pl
from jax.experimental.pallas import tpu as pltpu

F32 = jnp.float32
BF16 = jnp.bfloat16

D_MODEL = 1024
EPS = 1e-6
LN_EPS = 1e-5
CONV_WIDTH = 512
CONV_K = 31
N_HEADS = 8
QK_NOPE_DIM = 64
QK_ROPE_DIM = 32
ROPE_HALF = QK_ROPE_DIM // 2
V_HEAD_DIM = 64
Q_LORA_RANK = 256
KV_LORA_RANK = 128
ROPE_BASE = 10000.0
ATTN_WIDTH = N_HEADS * V_HEAD_DIM
SSM_WIDTH = 512
SSM_GROUP = 16
SSM_GROUPS = SSM_WIDTH // SSM_GROUP
SSM_STATE = 64
D_FF = 2816
FFN_K = 3

LANES = 128
SUBLANES = 8
V_ROWS = V_HEAD_DIM + 16
HEAD_PAD = LANES
QK_WIDTH = N_HEADS * HEAD_PAD
CONV_HALO = 32
CONV_ROWS = 64
FFN_HALO = SUBLANES
FFN_CHUNK = 256
FFN_NCHUNK = D_FF // FFN_CHUNK
SSM_L = 8
SSM_PAIRS = SSM_GROUPS // 2
SSM_PAIR_CH = 2 * SSM_GROUP
SSM_TILE = SSM_L * SSM_PAIR_CH
SSM_ROW = SSM_L * SSM_WIDTH
VMEM_LIMIT = 56 * 1024 * 1024


def _cparams(sem):
    return pltpu.CompilerParams(dimension_semantics=sem, vmem_limit_bytes=VMEM_LIMIT)


def _rms(x, g):
    return x * lax.rsqrt(jnp.mean(x * x, axis=-1, keepdims=True) + EPS) * g


def _sigmoid(x):
    return 1.0 / (1.0 + jnp.exp(-x))


def _rope128(t, cos, sin_up, sin_dn):
    return (t * cos + pltpu.roll(t, ROPE_HALF, axis=1) * sin_up
            + pltpu.roll(t, LANES - ROPE_HALF, axis=1) * sin_dn)


def _rope_unpack(packed):
    lane = lax.broadcasted_iota(jnp.int32, packed.shape, 1)
    r1, r2 = QK_NOPE_DIM, QK_NOPE_DIM + ROPE_HALF
    cos = jnp.where(lane < r1, 1.0, packed)
    up = pltpu.roll(packed, r2, axis=1)
    dn = pltpu.roll(packed, r1 - ROPE_HALF, axis=1)
    sin_up = jnp.where(lane >= r2, jnp.where(lane < r2 + ROPE_HALF, up, 0.0), 0.0)
    sin_dn = jnp.where(lane >= r1, jnp.where(lane < r2, dn, 0.0), 0.0)
    return cos, sin_up, sin_dn


def _pre0_kernel(x_ref, g_ref, w_in_ref, rope_ref, cw_ref, cb_ref, lng_ref, lnb_ref,
                 qn_ref, kvn_ref, wuq_ref, wukv_ref,
                 u_ref, q_ref, k_ref, vt_ref, ubuf, ushift, h_scr, q_scr, kv_scr, *, tm):
    sb = pl.program_id(1)

    @pl.when(sb == 0)
    def _():
        ubuf[0:CONV_HALO, :] = jnp.zeros((CONV_HALO, CONV_WIDTH), F32)

    hm = tm // 2
    n_in = w_in_ref.shape[1] // (2 * LANES)
    base = CONV_HALO - (CONV_K - 1)

    def in_proj(half):
        xn = _rms(x_ref[0, half * hm:(half + 1) * hm, :], g_ref[...]).astype(BF16)

        def piece(n):
            cols = slice(n * 2 * LANES, (n + 1) * 2 * LANES)
            h_scr[half, :, cols] = jnp.dot(xn, w_in_ref[:, cols], preferred_element_type=F32)
        return [functools.partial(piece, n) for n in range(n_in)]

    def qkv_proj(half):
        h = h_scr.at[half]
        o = 2 * CONV_WIDTH

        def q_piece():
            cq = _rms(h[:, o:o + Q_LORA_RANK], qn_ref[...]).astype(BF16)
            q_scr[...] = jnp.dot(cq, wuq_ref[...], preferred_element_type=F32)

        def kv_piece():
            o2 = o + Q_LORA_RANK
            ckv = _rms(h[:, o2:o2 + KV_LORA_RANK], kvn_ref[...]).astype(BF16)
            kv_scr[...] = jnp.dot(ckv, wukv_ref[...], preferred_element_type=F32)
        return [q_piece, kv_piece]

    def conv(half, between):
        h = h_scr.at[half]
        t0 = half * hm
        ubuf[CONV_HALO + t0:CONV_HALO + t0 + hm, :] = h[:, :CONV_WIDTH] * _sigmoid(h[:, CONV_WIDTH:2 * CONV_WIDTH])
        span = hm + CONV_HALO - SUBLANES
        for r in range(1, SUBLANES):
            ushift[r - 1, t0:t0 + span, :] = ubuf[t0 + r:t0 + r + span, :]
        chunks = range(t0, t0 + hm, CONV_ROWS)
        for i, r0 in enumerate(chunks):
            acc = jnp.broadcast_to(cb_ref[...], (CONV_ROWS, CONV_WIDTH))
            for o in range(base, base + CONV_K):
                r, al = o % SUBLANES, o - o % SUBLANES + r0
                tap = ushift[r - 1, al:al + CONV_ROWS, :] if r else ubuf[al:al + CONV_ROWS, :]
                acc = acc + cw_ref[o - base:o - base + 1, :] * tap
            mu = jnp.mean(acc, axis=-1, keepdims=True)
            cen = acc - mu
            var = jnp.mean(cen * cen, axis=-1, keepdims=True)
            y = cen * lax.rsqrt(var + LN_EPS) * lng_ref[...] + lnb_ref[...]
            u_ref[0, r0:r0 + CONV_ROWS, :] = (y * _sigmoid(y)).astype(BF16)
            for thunk in between[i * len(between) // len(chunks):(i + 1) * len(between) // len(chunks)]:
                thunk()

    def rotary_out(half):
        rows = slice(half * hm, (half + 1) * hm)
        cos, sin_up, sin_dn = _rope_unpack(rope_ref[rows, :])
        o = 2 * CONV_WIDTH + Q_LORA_RANK + KV_LORA_RANK
        kr = _rope128(h_scr[half, :, o:o + LANES], cos, sin_up, sin_dn)
        for hd in range(N_HEADS):
            sl = slice(hd * HEAD_PAD, (hd + 1) * HEAD_PAD)
            q_ref[0, rows, sl] = _rope128(q_scr[:, sl], cos, sin_up, sin_dn).astype(BF16)
            k_ref[0, rows, sl] = (kv_scr[:, sl] + kr).astype(BF16)
        vt = kv_scr[:, QK_WIDTH:].T
        ones = jnp.ones((V_ROWS - V_HEAD_DIM, hm), F32)
        parts = [blk for hd in range(N_HEADS) for blk in (vt[hd * V_HEAD_DIM:(hd + 1) * V_HEAD_DIM], ones)]
        vt_ref[0, 0, :, rows] = jnp.concatenate(parts, axis=0).astype(BF16)

    for thunk in in_proj(0):
        thunk()
    conv(0, in_proj(1))
    conv(1, qkv_proj(0))
    rotary_out(0)
    for thunk in qkv_proj(1):
        thunk()
    rotary_out(1)
    ubuf[0:CONV_HALO, :] = ubuf[tm:tm + CONV_HALO, :]


def _pre0(x, g, w_in, rope, cw, cb, lng, lnb, qn, kvn, wuq, wukv, *, tm):
    b, s, d = x.shape
    const = lambda shape: pl.BlockSpec(shape, lambda i, j: (0,) * len(shape))
    row = lambda w: pl.BlockSpec((1, tm, w), lambda i, j: (i, j, 0))
    return pl.pallas_call(
        functools.partial(_pre0_kernel, tm=tm),
        grid=(b, s // tm),
        in_specs=[row(d), const(g.shape), const(w_in.shape),
                  pl.BlockSpec((tm, LANES), lambda i, j: (j, 0)),
                  const(cw.shape), const(cb.shape), const(lng.shape), const(lnb.shape),
                  const(qn.shape), const(kvn.shape), const(wuq.shape), const(wukv.shape)],
        out_specs=[row(CONV_WIDTH), row(QK_WIDTH), row(QK_WIDTH),
                   pl.BlockSpec((1, 1, N_HEADS * V_ROWS, tm), lambda i, j: (i, j, 0, 0))],
        out_shape=[jax.ShapeDtypeStruct((b, s, CONV_WIDTH), BF16),
                   jax.ShapeDtypeStruct((b, s, QK_WIDTH), BF16),
                   jax.ShapeDtypeStruct((b, s, QK_WIDTH), BF16),
                   jax.ShapeDtypeStruct((b, s // tm, N_HEADS * V_ROWS, tm), BF16)],
        scratch_shapes=[pltpu.VMEM((tm + CONV_HALO, CONV_WIDTH), F32),
                        pltpu.VMEM((SUBLANES - 1, tm + CONV_HALO, CONV_WIDTH), F32),
                        pltpu.VMEM((2, tm // 2, w_in.shape[1]), F32),
                        pltpu.VMEM((tm // 2, wuq.shape[1]), F32),
                        pltpu.VMEM((tm // 2, wukv.shape[1]), F32)],
        compiler_params=_cparams(("arbitrary", "arbitrary")),
        name="pre0",
    )(x, g, w_in, rope, cw, cb, lng, lnb, qn, kvn, wuq, wukv)


NEG_BIG = -1e30
ATTN_HEADS = 4
KV_UNROLL = 4


def _attn_kernel(q_ref, k_ref, vt_ref, o_ref, *s_refs, t):
    qi = pl.program_id(2)
    nh = len(s_refs)

    def scores(j, hh):
        lanes = slice(hh * HEAD_PAD, (hh + 1) * HEAD_PAD)
        k = k_ref[0, pl.ds(pl.multiple_of(j * t, t), t), lanes]
        s_refs[hh][...] = lax.dot_general(k, q_ref[0, :, lanes], (((1,), (1,)), ((), ())),
                                          preferred_element_type=F32)

    def update(j, hh, state, masked):
        m, acc = state
        st = s_refs[hh][...]
        if masked:
            key = lax.broadcasted_iota(jnp.int32, (t, t), 0)
            qry = lax.broadcasted_iota(jnp.int32, (t, t), 1)
            st = jnp.where(key <= qry, st, NEG_BIG)
        m_new = jnp.maximum(m, jnp.max(st, axis=0, keepdims=True))
        alpha = jnp.exp2(m - m_new)
        p = jnp.exp2(st - m_new).astype(BF16)
        vt = vt_ref[0, j, hh * V_ROWS:(hh + 1) * V_ROWS, :]
        acc = alpha * acc + jnp.dot(vt, p, preferred_element_type=F32)
        return m_new, acc

    def block(j, carry, masked):
        out = ()
        for hh in range(nh):
            if hh + 1 < nh:
                scores(j, hh + 1)
            elif not masked:
                scores(j + 1, 0)
            out += update(j, hh, carry[2 * hh:2 * hh + 2], masked)
        return out

    body = functools.partial(block, masked=False)

    def unrolled(i, c):
        for u in range(KV_UNROLL):
            c = body(KV_UNROLL * i + u, c)
        return c

    init = (jnp.full((1, t), NEG_BIG, F32), jnp.zeros((V_ROWS, t), F32)) * nh
    scores(0, 0)
    carry = lax.fori_loop(0, qi // KV_UNROLL, unrolled, init)
    carry = lax.fori_loop(qi // KV_UNROLL * KV_UNROLL, qi, body, carry)
    carry = block(qi, carry, True)
    d = V_HEAD_DIM
    ot = jnp.concatenate([carry[2 * hh + 1][:d] / carry[2 * hh + 1][d:d + 1] for hh in range(nh)], axis=0)
    o_ref[0] = ot.T.astype(o_ref.dtype)


def _attention(q, k, vt, *, t):
    b, s, _ = q.shape
    assert vt.shape == (b, s // t, N_HEADS * V_ROWS, t)
    nh = ATTN_HEADS
    return pl.pallas_call(
        functools.partial(_attn_kernel, t=t),
        grid=(b, N_HEADS // nh, s // t),
        in_specs=[pl.BlockSpec((1, t, nh * HEAD_PAD), lambda i, h, j: (i, j, h)),
                  pl.BlockSpec((1, s, nh * HEAD_PAD), lambda i, h, j: (i, 0, h)),
                  pl.BlockSpec((1, s // t, nh * V_ROWS, t), lambda i, h, j: (i, 0, h, 0))],
        out_specs=pl.BlockSpec((1, t, nh * V_HEAD_DIM), lambda i, h, j: (i, j, h)),
        out_shape=jax.ShapeDtypeStruct((b, s, ATTN_WIDTH), BF16),
        scratch_shapes=[pltpu.VMEM((t, t), F32)] * nh,
        compiler_params=_cparams(("arbitrary", "arbitrary", "arbitrary")),
        name="attn",
    )(q, k, vt)


def _ffn_kernel(*refs, tm, mix_out, final_norm):
    if mix_out:
        x_ref, u_ref, a_ref, wu_ref, wa_ref, *refs = refs
    else:
        x_ref, *refs = refs
    g_ref, wup_ref, cw_ref, wd_ref, fg_ref, o_ref, xn_scr, hbuf0, hbuf1, abuf0, abuf1, carry = refs
    sb = pl.program_id(1)
    nsub = x_ref.shape[1] // tm
    halves = lambda c: ((slice(0, FFN_CHUNK), c * FFN_CHUNK),
                        (slice(FFN_CHUNK, 2 * FFN_CHUNK), D_FF + c * FFN_CHUNK))

    def start(r):
        rows = slice(r * tm, (r + 1) * tm)
        x = x_ref[0, rows, :]
        if mix_out:
            x = (x + jnp.dot(u_ref[0, rows, :], wu_ref[...], preferred_element_type=F32)
                 + jnp.dot(a_ref[0, rows, :], wa_ref[...], preferred_element_type=F32))
        xn_scr[r] = _rms(x, g_ref[...]).astype(BF16)
        o_ref[0, rows, :] = x

    def up(r, c, hbuf):
        first = sb == 0 if r == 0 else False
        for lanes, col in halves(c):
            h = jnp.dot(xn_scr[r], wup_ref[:, col:col + FFN_CHUNK], preferred_element_type=F32)
            hbuf[0:FFN_HALO, lanes] = jnp.where(first, 0.0, carry[c, :, lanes])
            hbuf[FFN_HALO:FFN_HALO + tm, lanes] = h
            carry[c, :, lanes] = h[tm - FFN_HALO:, :]

    def gate(c, hbuf, abuf):
        cv = []
        for lanes, col in halves(c):
            w = cw_ref[:, col:col + FFN_CHUNK]
            v = w[FFN_K - 1:FFN_K, :] * hbuf[FFN_HALO:FFN_HALO + tm, lanes] + w[FFN_K:FFN_K + 1, :]
            for k in range(FFN_K - 1):
                off = FFN_HALO - (FFN_K - 1) + k
                v = v + w[k:k + 1, :] * hbuf[off:off + tm, lanes]
            cv.append(v)
        abuf[...] = (cv[0] * _sigmoid(cv[0]) * cv[1]).astype(BF16)

    def down(r, c, abuf):
        rows = slice(r * tm, (r + 1) * tm)
        o_ref[0, rows, :] += jnp.dot(abuf[...], wd_ref[c * FFN_CHUNK:(c + 1) * FFN_CHUNK, :],
                                     preferred_element_type=F32)
        if final_norm and c == FFN_NCHUNK - 1:
            o_ref[0, rows, :] = _rms(o_ref[0, rows, :], fg_ref[...])

    hb, ab = (hbuf0, hbuf1), (abuf0, abuf1)
    items = [(r, c) for r in range(nsub) for c in range(FFN_NCHUNK)]
    start(0)
    up(0, 0, hb[0])
    for i, (r, c) in enumerate(items):
        if i + 1 < len(items):
            rn, cn = items[i + 1]
            if cn == 0:
                start(rn)
            up(rn, cn, hb[(i + 1) % 2])
        if i:
            down(*items[i - 1], ab[(i - 1) % 2])
        gate(c, hb[i % 2], ab[i % 2])
    down(*items[-1], ab[(len(items) - 1) % 2])


def _ffn(x, mix, g, wup, cw, wd, fg, *, tm, nsub, final_norm):
    b, s, d = x.shape
    const = lambda a: pl.BlockSpec(a.shape, lambda i, j: (0,) * a.ndim, pipeline_mode=pl.Buffered(1))
    row = lambda w: pl.BlockSpec((1, nsub * tm, w), lambda i, j: (i, j, 0))
    ins, specs = [x], [row(d)]
    if mix is not None:
        u, a, wu, wa = mix
        ins += [u, a, wu, wa]
        specs += [row(u.shape[-1]), row(a.shape[-1]), const(wu), const(wa)]
    ins += [g, wup, cw, wd, fg]
    specs += [const(g), const(wup), const(cw), const(wd), const(fg)]
    return pl.pallas_call(
        functools.partial(_ffn_kernel, tm=tm, mix_out=mix is not None, final_norm=final_norm),
        grid=(b, s // (nsub * tm)),
        in_specs=specs,
        out_specs=row(d),
        out_shape=jax.ShapeDtypeStruct((b, s, d), F32),
        scratch_shapes=[pltpu.VMEM((nsub, tm, d), BF16),
                        pltpu.VMEM((tm + FFN_HALO, 2 * FFN_CHUNK), F32),
                        pltpu.VMEM((tm + FFN_HALO, 2 * FFN_CHUNK), F32),
                        pltpu.VMEM((tm, FFN_CHUNK), BF16),
                        pltpu.VMEM((tm, FFN_CHUNK), BF16),
                        pltpu.VMEM((FFN_NCHUNK, FFN_HALO, 2 * FFN_CHUNK), F32)],
        compiler_params=_cparams(("arbitrary", "arbitrary")),
        name="ffn_final" if final_norm else "ffn",
    )(*ins)


def _regroup(slabs, n_out, pieces):
    rows = slabs[0].shape[0]
    quarter = lax.broadcasted_iota(jnp.int32, (rows, LANES), 1) // SSM_PAIR_CH
    out = []
    for d in range(n_out):
        acc = None
        for dst_q, (src, src_q) in enumerate(pieces(d)):
            shift = ((dst_q - src_q) * SSM_PAIR_CH) % LANES
            piece = pltpu.roll(slabs[src], shift, axis=1) if shift else slabs[src]
            acc = piece if acc is None else jnp.where(quarter == dst_q, piece, acc)
        out.append(acc)
    return out


def _shift_rows(x, n, row):
    return jnp.where(row >= n, pltpu.roll(x, n, axis=0), 0.0)


def _mix1_kernel(x_ref, g_ref, win_ref, mb_ref, t_ref, mc_ref, lp_ref, d_ref, wglu_ref, bglu_ref,
                 o_ref, carry, u_slab, y_slab, *, rows, nlev):
    sb = pl.program_id(1)

    @pl.when(sb == 0)
    def _():
        carry[...] = jnp.zeros(carry.shape, F32)

    x = x_ref[0]
    xn = _rms(x, g_ref[...]).astype(BF16)
    u = jnp.dot(xn, win_ref[...], preferred_element_type=F32)
    nch = SSM_WIDTH // LANES
    for s in range(nch):
        u_slab[s] = u[:, s * LANES:(s + 1) * LANES]
    nslab = SSM_ROW // LANES
    u_slabs = [u_slab[i % nch, pl.ds(i // nch, rows, stride=SSM_L), :] for i in range(nslab)]
    a_slabs = _regroup(u_slabs, nslab,
                       lambda d: [(4 * (4 * (d % 2) + i) + (d // 2) // 4, (d // 2) % 4) for i in range(4)])
    row = lax.broadcasted_iota(jnp.int32, (rows, LANES), 0)
    first = row == 0
    y_slabs = []
    for p in range(SSM_PAIRS):
        a = jnp.concatenate(a_slabs[2 * p:2 * p + 2], axis=1)
        a16 = a.astype(BF16)
        lo = slice(p * SSM_TILE, p * SSM_TILE + LANES)
        hi = slice(p * SSM_TILE + LANES, (p + 1) * SSM_TILE)
        sloc = jnp.dot(a16, mb_ref[p], preferred_element_type=F32)
        sr, si = sloc[:, :LANES], sloc[:, LANES:]
        cr, ci = carry[0:1, lo], carry[0:1, hi]
        lr, li = lp_ref[0:1, lo], lp_ref[0:1, hi]
        sr = sr + jnp.where(first, lr * cr - li * ci, 0.0)
        si = si + jnp.where(first, lr * ci + li * cr, 0.0)
        for lev in range(nlev):
            n = 1 << lev
            lr, li = lp_ref[lev:lev + 1, lo], lp_ref[lev:lev + 1, hi]
            pr, pi = _shift_rows(sr, n, row), _shift_rows(si, n, row)
            sr, si = sr + (lr * pr - li * pi), si + (lr * pi + li * pr)
        xr = jnp.where(first, cr, pltpu.roll(sr, 1, axis=0))
        xi = jnp.where(first, ci, pltpu.roll(si, 1, axis=0))
        carry[0:1, lo] = sr[rows - 1:rows, :]
        carry[0:1, hi] = si[rows - 1:rows, :]
        xp = jnp.concatenate([xr, xi], axis=1).astype(BF16)
        y = (jnp.dot(a16, t_ref[p], preferred_element_type=F32)
             + jnp.dot(xp, mc_ref[p], preferred_element_type=F32)
             + d_ref[:, p * SSM_TILE:(p + 1) * SSM_TILE] * a)
        y = jax.nn.gelu(y, approximate=True)
        y_slabs += [y[:, :LANES], y[:, LANES:]]
    o_slabs = _regroup(y_slabs, nslab,
                       lambda s: [(2 * (4 * (s % 4) + j) + (s // 4) // 4, (s // 4) % 4) for j in range(4)])
    for i in range(nslab):
        y_slab[i % nch, pl.ds(i // nch, rows, stride=SSM_L), :] = o_slabs[i]
    y = jnp.concatenate([y_slab[s] for s in range(nch)], axis=1).astype(BF16)
    z = jnp.dot(y, wglu_ref[...], preferred_element_type=F32) + bglu_ref[...]
    o_ref[0] = x + z[:, :D_MODEL] * _sigmoid(z[:, D_MODEL:])


def _mix1(x, g, win, mb, tt, mc, lp, dd, wglu, bglu, *, rows):
    b, s, d = x.shape
    tok = rows * SSM_L
    nlev = int(math.log2(rows))
    assert 1 << nlev == rows and lp.shape[0] >= nlev
    const = lambda a: pl.BlockSpec(a.shape, lambda i, j: (0,) * a.ndim, pipeline_mode=pl.Buffered(1))
    return pl.pallas_call(
        functools.partial(_mix1_kernel, rows=rows, nlev=nlev),
        grid=(b, s // tok),
        in_specs=[pl.BlockSpec((1, tok, d), lambda i, j: (i, j, 0)),
                  const(g), const(win), const(mb), const(tt), const(mc), const(lp), const(dd),
                  const(wglu), const(bglu)],
        out_specs=pl.BlockSpec((1, tok, d), lambda i, j: (i, j, 0)),
        out_shape=jax.ShapeDtypeStruct((b, s, d), F32),
        scratch_shapes=[pltpu.VMEM((SUBLANES, SSM_ROW), F32),
                        pltpu.VMEM((SSM_WIDTH // LANES, tok, LANES), F32),
                        pltpu.VMEM((SSM_WIDTH // LANES, tok, LANES), F32)],
        compiler_params=_cparams(("arbitrary", "arbitrary")),
        name="mix1",
    )(x, g, win, mb, tt, mc, lp, dd, wglu, bglu)


def _ssm_tables(log_dt, a_re, a_im, b_re, b_im, c_re, c_im, d_skip, nlev):
    g, p, c, L = SSM_GROUPS, SSM_STATE, SSM_GROUP, SSM_L
    dt = jnp.exp(log_dt.astype(F32))[:, None]
    ar, ai = a_re.astype(F32), a_im.astype(F32)
    mag = jnp.exp(ar * dt)
    lb_re, lb_im = mag * jnp.cos(ai * dt), mag * jnp.sin(ai * dt)
    den = ar * ar + ai * ai
    nr, ni = lb_re - 1.0, lb_im
    f_re = (nr * ar + ni * ai) / den
    f_im = (ni * ar - nr * ai) / den
    br, bi = b_re.astype(F32), b_im.astype(F32)
    bb_re = f_re[..., None] * br - f_im[..., None] * bi
    bb_im = f_re[..., None] * bi + f_im[..., None] * br

    def lam_pow(n):
        n = jnp.asarray(n, F32)[:, None, None]
        m = jnp.exp(n * (ar * dt))
        return m * jnp.cos(n * (ai * dt)), m * jnp.sin(n * (ai * dt))

    cr, ci = c_re.astype(F32), c_im.astype(F32)
    pr, pi = lam_pow(jnp.arange(L + 1))
    lbr = pr[..., None] * bb_re[None] - pi[..., None] * bb_im[None]
    lbi = pr[..., None] * bb_im[None] + pi[..., None] * bb_re[None]
    kk = (jnp.einsum('gop,ngpc->ngoc', cr, lbr[:L]) - jnp.einsum('gop,ngpc->ngoc', ci, lbi[:L]))
    cat = jnp.concatenate
    q = SSM_PAIRS

    def pair_diag(a0, a1):
        z = jnp.zeros_like(a0)
        return cat([cat([a0, z], axis=2), cat([z, a1], axis=2)], axis=1)

    kt = kk.transpose(0, 1, 3, 2).reshape(L, q, 2, c, c)
    kbd = [pair_diag(kt[n, :, 0], kt[n, :, 1]) for n in range(L)]
    zero = jnp.zeros_like(kbd[0])
    tt = cat([cat([kbd[t2 - t1] if t2 >= t1 else zero for t2 in range(L)], axis=2) for t1 in range(L)], axis=1)
    def mb_rows(n):
        r = lbr[n].transpose(0, 2, 1).reshape(q, 2, c, p)
        i = lbi[n].transpose(0, 2, 1).reshape(q, 2, c, p)
        return cat([pair_diag(r[:, 0], r[:, 1]), pair_diag(i[:, 0], i[:, 1])], axis=2)
    mb = cat([mb_rows(L - 1 - t) for t in range(L)], axis=1)
    p1r, p1i = pr[1:], pi[1:]
    clr = cr[None] * p1r[:, :, None, :] - ci[None] * p1i[:, :, None, :]
    cli = cr[None] * p1i[:, :, None, :] + ci[None] * p1r[:, :, None, :]
    def mc_cols(t):
        r = clr[t].transpose(0, 2, 1).reshape(q, 2, p, c)
        i = -cli[t].transpose(0, 2, 1).reshape(q, 2, p, c)
        return cat([pair_diag(r[:, 0], r[:, 1]), pair_diag(i[:, 0], i[:, 1])], axis=1)
    mc = cat([mc_cols(t) for t in range(L)], axis=2)
    sr, si = lam_pow(L * (2 ** jnp.arange(nlev)))
    lp = jnp.stack([sr.reshape(nlev, SSM_PAIRS, 2 * p), si.reshape(nlev, SSM_PAIRS, 2 * p)], axis=2)
    lp = lp.reshape(nlev, SSM_ROW)
    lp = jnp.pad(lp, ((0, (-nlev) % SUBLANES), (0, 0)))
    dd = jnp.broadcast_to(d_skip.astype(F32).reshape(SSM_PAIRS, 1, SSM_PAIR_CH),
                          (SSM_PAIRS, L, SSM_PAIR_CH)).reshape(1, SSM_ROW)
    return mb.astype(BF16), tt.astype(BF16), mc.astype(BF16), lp, dd


def _head_pad_cols(nope, rope):
    k = nope.shape[0]
    parts = [nope]
    if rope is not None:
        parts.append(rope)
    used = sum(t.shape[-1] for t in parts)
    parts.append(jnp.zeros((k, N_HEADS, HEAD_PAD - used), nope.dtype))
    return jnp.concatenate(parts, axis=-1).reshape(k, QK_WIDTH)


def _rope_tables(s):
    inv = ROPE_BASE ** (-jnp.arange(ROPE_HALF, dtype=F32) / ROPE_HALF)
    ang = jnp.arange(s, dtype=F32)[:, None] * inv[None, :]
    cos, sin = jnp.cos(ang), jnp.sin(ang)
    gap = jnp.zeros((s, QK_NOPE_DIM - QK_ROPE_DIM), F32)
    tail = jnp.ones((s, HEAD_PAD - QK_NOPE_DIM - QK_ROPE_DIM), F32)
    return jnp.concatenate([sin, -sin, gap, cos, cos, tail], axis=1)


def _pick(total, want):
    t = min(total, want)
    assert total % t == 0
    return t


def kernel(x, l0_mix_norm, l0_w_in, l0_conv_w, l0_conv_b, l0_conv_ln_g, l0_conv_ln_b, l0_q_norm, l0_kv_norm, l0_w_uq, l0_w_ukv, l0_w_out, l0_ffn_norm, l0_w_up, l0_ffn_conv_w, l0_ffn_conv_b, l0_w_down, l1_mix_norm, l1_w_in, l1_log_dt, l1_a_re, l1_a_im, l1_b_re, l1_b_im, l1_c_re, l1_c_im, l1_d, l1_w_glu, l1_b_glu, l1_ffn_norm, l1_w_up, l1_ffn_conv_w, l1_ffn_conv_b, l1_w_down, final_norm):
    b, s, d = x.shape
    row2 = lambda v: v.reshape(1, -1).astype(F32)

    o = 2 * CONV_WIDTH + Q_LORA_RANK + KV_LORA_RANK
    w_kr = l0_w_in[:, o:]
    kr_blk = jnp.concatenate([jnp.zeros((d, QK_NOPE_DIM), F32), w_kr,
                              jnp.zeros((d, HEAD_PAD - QK_NOPE_DIM - QK_ROPE_DIM), F32)], axis=1)
    w_in0 = jnp.concatenate([l0_w_in[:, :o], kr_blk], axis=1).astype(BF16)
    scale = (QK_NOPE_DIM + QK_ROPE_DIM) ** -0.5 * math.log2(math.e)
    wq = (l0_w_uq * scale).reshape(Q_LORA_RANK, N_HEADS, QK_NOPE_DIM + QK_ROPE_DIM)
    wuq = _head_pad_cols(wq[..., :QK_NOPE_DIM], wq[..., QK_NOPE_DIM:]).astype(BF16)
    wkv = l0_w_ukv.reshape(KV_LORA_RANK, N_HEADS, QK_NOPE_DIM + V_HEAD_DIM)
    wukv = jnp.concatenate([_head_pad_cols(wkv[..., :QK_NOPE_DIM], None),
                            wkv[..., QK_NOPE_DIM:].reshape(KV_LORA_RANK, ATTN_WIDTH)], axis=1).astype(BF16)
    cw0 = jnp.pad(l0_conv_w.astype(F32), ((0, CONV_HALO - CONV_K), (0, 0)))
    rope = _rope_tables(s)

    t_attn = _pick(s, 512)
    u0, q, k, vt = _pre0(x, row2(l0_mix_norm), w_in0, rope, cw0, row2(l0_conv_b),
                         row2(l0_conv_ln_g), row2(l0_conv_ln_b), row2(l0_q_norm), row2(l0_kv_norm),
                         wuq, wukv, tm=t_attn)
    attn = _attention(q, k, vt, t=t_attn)
    w_out = l0_w_out.astype(BF16)
    mix0 = (u0, attn, w_out[:CONV_WIDTH], w_out[CONV_WIDTH:])

    tm_ffn = _pick(s, 512)

    def ffn(xx, mix, g, w_up, cw, cb, w_down, final):
        taps = jnp.concatenate([cw.astype(F32), cb.astype(F32)[None],
                                jnp.zeros((SUBLANES - FFN_K - 1, 2 * D_FF), F32)], axis=0)
        return _ffn(xx, mix, row2(g), w_up.astype(BF16), taps, w_down.astype(BF16),
                    row2(final_norm), tm=tm_ffn, nsub=1, final_norm=final)

    x2 = ffn(x, mix0, l0_ffn_norm, l0_w_up, l0_ffn_conv_w, l0_ffn_conv_b, l0_w_down, False)

    rows = _pick(s // SSM_L, 128)
    nlev = int(math.log2(rows))
    mb, tt, mc, lp, dd = _ssm_tables(l1_log_dt, l1_a_re, l1_a_im, l1_b_re, l1_b_im,
                                     l1_c_re, l1_c_im, l1_d, nlev)
    x3 = _mix1(x2, row2(l1_mix_norm), l1_w_in.astype(BF16), mb, tt, mc, lp, dd,
               l1_w_glu.astype(BF16), row2(l1_b_glu), rows=rows)
    return ffn(x3, None, l1_ffn_norm, l1_w_up, l1_ffn_conv_w, l1_ffn_conv_b, l1_w_down, True)
```

```python
import functools
import math

import jax
import jax.numpy as jnp
from jax import lax
from jax.experimental import pallas as pl
from jax.experimental.pallas import tpu as pltpu

F32 = jnp.float32
BF16 = jnp.bfloat16

D_MODEL = 1024
EPS = 1e-6
LN_EPS = 1e-5
CONV_WIDTH = 512
CONV_K = 31
N_HEADS = 8
QK_NOPE_DIM = 64
QK_ROPE_DIM = 32
ROPE_HALF = QK_ROPE_DIM // 2
V_HEAD_DIM = 64
Q_LORA_RANK = 256
KV_LORA_RANK = 128
ROPE_BASE = 10000.0
ATTN_WIDTH = N_HEADS * V_HEAD_DIM
SSM_WIDTH = 512
SSM_GROUP = 16
SSM_GROUPS = SSM_WIDTH // SSM_GROUP
SSM_STATE = 64
D_FF = 2816
FFN_K = 3

LANES = 128
SUBLANES = 8
V_ROWS = V_HEAD_DIM + 16
HEAD_PAD = LANES
QK_WIDTH = N_HEADS * HEAD_PAD
CONV_HALO = 32
CONV_ROWS = 64
FFN_HALO = SUBLANES
FFN_CHUNK = 256
FFN_NCHUNK = D_FF // FFN_CHUNK
SSM_L = 8
SSM_PAIRS = SSM_GROUPS // 2
SSM_PAIR_CH = 2 * SSM_GROUP
SSM_TILE = SSM_L * SSM_PAIR_CH
SSM_ROW = SSM_L * SSM_WIDTH
VMEM_LIMIT = 56 * 1024 * 1024


def _cparams(sem):
    return pltpu.CompilerParams(dimension_semantics=sem, vmem_limit_bytes=VMEM_LIMIT)


def _rms(x, g):
    return x * lax.rsqrt(jnp.mean(x * x, axis=-1, keepdims=True) + EPS) * g


def _sigmoid(x):
    return 1.0 / (1.0 + jnp.exp(-x))


def _rope128(t, cos, sin_up, sin_dn):
    return (t * cos + pltpu.roll(t, ROPE_HALF, axis=1) * sin_up
            + pltpu.roll(t, LANES - ROPE_HALF, axis=1) * sin_dn)


def _rope_unpack(packed):
    lane = lax.broadcasted_iota(jnp.int32, packed.shape, 1)
    r1, r2 = QK_NOPE_DIM, QK_NOPE_DIM + ROPE_HALF
    cos = jnp.where(lane < r1, 1.0, packed)
    up = pltpu.roll(packed, r2, axis=1)
    dn = pltpu.roll(packed, r1 - ROPE_HALF, axis=1)
    sin_up = jnp.where(lane >= r2, jnp.where(lane < r2 + ROPE_HALF, up, 0.0), 0.0)
    sin_dn = jnp.where(lane >= r1, jnp.where(lane < r2, dn, 0.0), 0.0)
    return cos, sin_up, sin_dn


def _pre0_kernel(x_ref, g_ref, w_in_ref, rope_ref, cw_ref, cb_ref, lng_ref, lnb_ref,
                 qn_ref, kvn_ref, wuq_ref, wukv_ref,
                 u_ref, q_ref, k_ref, vt_ref, ubuf, ushift, h_scr, q_scr, kv_scr, *, tm):
    sb = pl.program_id(1)

    @pl.when(sb == 0)
    def _():
        ubuf[0:CONV_HALO, :] = jnp.zeros((CONV_HALO, CONV_WIDTH), F32)

    hm = tm // 2
    n_in = w_in_ref.shape[1] // (2 * LANES)
    base = CONV_HALO - (CONV_K - 1)

    def in_proj(half):
        xn = _rms(x_ref[0, half * hm:(half + 1) * hm, :], g_ref[...]).astype(BF16)

        def piece(n):
            cols = slice(n * 2 * LANES, (n + 1) * 2 * LANES)
            h_scr[half, :, cols] = jnp.dot(xn, w_in_ref[:, cols], preferred_element_type=F32)
        return [functools.partial(piece, n) for n in range(n_in)]

    def qkv_proj(half):
        h = h_scr.at[half]
        o = 2 * CONV_WIDTH

        def q_piece():
            cq = _rms(h[:, o:o + Q_LORA_RANK], qn_ref[...]).astype(BF16)
            q_scr[...] = jnp.dot(cq, wuq_ref[...], preferred_element_type=F32)

        def kv_piece():
            o2 = o + Q_LORA_RANK
            ckv = _rms(h[:, o2:o2 + KV_LORA_RANK], kvn_ref[...]).astype(BF16)
            kv_scr[...] = jnp.dot(ckv, wukv_ref[...], preferred_element_type=F32)
        return [q_piece, kv_piece]

    def conv(half, between):
        h = h_scr.at[half]
        t0 = half * hm
        ubuf[CONV_HALO + t0:CONV_HALO + t0 + hm, :] = h[:, :CONV_WIDTH] * _sigmoid(h[:, CONV_WIDTH:2 * CONV_WIDTH])
        span = hm + CONV_HALO - SUBLANES
        for r in range(1, SUBLANES):
            ushift[r - 1, t0:t0 + span, :] = ubuf[t0 + r:t0 + r + span, :]
        chunks = range(t0, t0 + hm, CONV_ROWS)
        for i, r0 in enumerate(chunks):
            acc = jnp.broadcast_to(cb_ref[...], (CONV_ROWS, CONV_WIDTH))
            for o in range(base, base + CONV_K):
                r, al = o % SUBLANES, o - o % SUBLANES + r0
                tap = ushift[r - 1, al:al + CONV_ROWS, :] if r else ubuf[al:al + CONV_ROWS, :]
                acc = acc + cw_ref[o - base:o - base + 1, :] * tap
            mu = jnp.mean(acc, axis=-1, keepdims=True)
            cen = acc - mu
            var = jnp.mean(cen * cen, axis=-1, keepdims=True)
            y = cen * lax.rsqrt(var + LN_EPS) * lng_ref[...] + lnb_ref[...]
            u_ref[0, r0:r0 + CONV_ROWS, :] = (y * _sigmoid(y)).astype(BF16)
            for thunk in between[i * len(between) // len(chunks):(i + 1) * len(between) // len(chunks)]:
                thunk()

    def rotary_out(half):
        rows = slice(half * hm, (half + 1) * hm)
        cos, sin_up, sin_dn = _rope_unpack(rope_ref[rows, :])
        o = 2 * CONV_WIDTH + Q_LORA_RANK + KV_LORA_RANK
        kr = _rope128(h_scr[half, :, o:o + LANES], cos, sin_up, sin_dn)
        sin = sin_up - sin_dn
        for hd in range(N_HEADS):
            sl = slice(hd * HEAD_PAD, (hd + 1) * HEAD_PAD)
            rot = slice(QK_WIDTH + hd * HEAD_PAD, QK_WIDTH + (hd + 1) * HEAD_PAD)
            q_ref[0, rows, sl] = (q_scr[:, sl] * cos + q_scr[:, rot] * sin).astype(BF16)
            k_ref[0, rows, sl] = (kv_scr[:, sl] + kr).astype(BF16)
        vt = kv_scr[:, QK_WIDTH:].T
        ones = jnp.ones((V_ROWS - V_HEAD_DIM, hm), F32)
        parts = [blk for hd in range(N_HEADS) for blk in (vt[hd * V_HEAD_DIM:(hd + 1) * V_HEAD_DIM], ones)]
        vt_ref[0, 0, :, rows] = jnp.concatenate(parts, axis=0).astype(BF16)

    for thunk in in_proj(0):
        thunk()
    conv(0, in_proj(1))
    conv(1, qkv_proj(0))
    rotary_out(0)
    for thunk in qkv_proj(1):
        thunk()
    rotary_out(1)
    ubuf[0:CONV_HALO, :] = ubuf[tm:tm + CONV_HALO, :]


def _pre0(x, g, w_in, rope, cw, cb, lng, lnb, qn, kvn, wuq, wukv, *, tm):
    b, s, d = x.shape
    const = lambda shape: pl.BlockSpec(shape, lambda i, j: (0,) * len(shape))
    row = lambda w: pl.BlockSpec((1, tm, w), lambda i, j: (i, j, 0))
    return pl.pallas_call(
        functools.partial(_pre0_kernel, tm=tm),
        grid=(b, s // tm),
        in_specs=[row(d), const(g.shape), const(w_in.shape),
                  pl.BlockSpec((tm, LANES), lambda i, j: (j, 0)),
                  const(cw.shape), const(cb.shape), const(lng.shape), const(lnb.shape),
                  const(qn.shape), const(kvn.shape), const(wuq.shape), const(wukv.shape)],
        out_specs=[row(CONV_WIDTH), row(QK_WIDTH), row(QK_WIDTH),
                   pl.BlockSpec((1, 1, N_HEADS * V_ROWS, tm), lambda i, j: (i, j, 0, 0))],
        out_shape=[jax.ShapeDtypeStruct((b, s, CONV_WIDTH), BF16),
                   jax.ShapeDtypeStruct((b, s, QK_WIDTH), BF16),
                   jax.ShapeDtypeStruct((b, s, QK_WIDTH), BF16),
                   jax.ShapeDtypeStruct((b, s // tm, N_HEADS * V_ROWS, tm), BF16)],
        scratch_shapes=[pltpu.VMEM((tm + CONV_HALO, CONV_WIDTH), F32),
                        pltpu.VMEM((SUBLANES - 1, tm + CONV_HALO, CONV_WIDTH), F32),
                        pltpu.VMEM((2, tm // 2, w_in.shape[1]), F32),
                        pltpu.VMEM((tm // 2, wuq.shape[1]), F32),
                        pltpu.VMEM((tm // 2, wukv.shape[1]), F32)],
        compiler_params=_cparams(("arbitrary", "arbitrary")),
        name="pre0",
    )(x, g, w_in, rope, cw, cb, lng, lnb, qn, kvn, wuq, wukv)


NEG_BIG = -1e30
ATTN_HEADS = 4
KV_UNROLL = 4


def _attn_kernel(q_ref, k_ref, vt_ref, o_ref, *s_refs, t):
    qi = pl.program_id(2)
    nh = len(s_refs)

    def scores(j, hh):
        lanes = slice(hh * HEAD_PAD, (hh + 1) * HEAD_PAD)
        k = k_ref[0, pl.ds(pl.multiple_of(j * t, t), t), lanes]
        s_refs[hh][...] = lax.dot_general(k, q_ref[0, :, lanes], (((1,), (1,)), ((), ())),
                                          preferred_element_type=F32)

    def update(j, hh, state, masked):
        m, acc = state
        st = s_refs[hh][...]
        if masked:
            key = lax.broadcasted_iota(jnp.int32, (t, t), 0)
            qry = lax.broadcasted_iota(jnp.int32, (t, t), 1)
            st = jnp.where(key <= qry, st, NEG_BIG)
        m_new = jnp.maximum(m, jnp.max(st, axis=0, keepdims=True))
        alpha = jnp.exp2(m - m_new)
        p = jnp.exp2(st - m_new).astype(BF16)
        vt = vt_ref[0, j, hh * V_ROWS:(hh + 1) * V_ROWS, :]
        acc = alpha * acc + jnp.dot(vt, p, preferred_element_type=F32)
        return m_new, acc

    def block(j, carry, masked):
        out = ()
        for hh in range(nh):
            if hh + 1 < nh:
                scores(j, hh + 1)
            elif not masked:
                scores(j + 1, 0)
            out += update(j, hh, carry[2 * hh:2 * hh + 2], masked)
        return out

    body = functools.partial(block, masked=False)

    def unrolled(i, c):
        for u in range(KV_UNROLL):
            c = body(KV_UNROLL * i + u, c)
        return c

    init = (jnp.full((1, t), NEG_BIG, F32), jnp.zeros((V_ROWS, t), F32)) * nh
    scores(0, 0)
    carry = lax.fori_loop(0, qi // KV_UNROLL, unrolled, init)
    carry = lax.fori_loop(qi // KV_UNROLL * KV_UNROLL, qi, body, carry)
    carry = block(qi, carry, True)
    d = V_HEAD_DIM
    ot = jnp.concatenate([carry[2 * hh + 1][:d] / carry[2 * hh + 1][d:d + 1] for hh in range(nh)], axis=0)
    o_ref[0] = ot.T.astype(o_ref.dtype)


def _attention(q, k, vt, *, t):
    b, s, _ = q.shape
    assert vt.shape == (b, s // t, N_HEADS * V_ROWS, t)
    nh = ATTN_HEADS
    return pl.pallas_call(
        functools.partial(_attn_kernel, t=t),
        grid=(b, N_HEADS // nh, s // t),
        in_specs=[pl.BlockSpec((1, t, nh * HEAD_PAD), lambda i, h, j: (i, j, h)),
                  pl.BlockSpec((1, s, nh * HEAD_PAD), lambda i, h, j: (i, 0, h)),
                  pl.BlockSpec((1, s // t, nh * V_ROWS, t), lambda i, h, j: (i, 0, h, 0))],
        out_specs=pl.BlockSpec((1, t, nh * V_HEAD_DIM), lambda i, h, j: (i, j, h)),
        out_shape=jax.ShapeDtypeStruct((b, s, ATTN_WIDTH), BF16),
        scratch_shapes=[pltpu.VMEM((t, t), F32)] * nh,
        compiler_params=_cparams(("arbitrary", "arbitrary", "arbitrary")),
        name="attn",
    )(q, k, vt)


def _ffn_kernel(*refs, tm, mix_out, final_norm):
    if mix_out:
        x_ref, u_ref, a_ref, wu_ref, wa_ref, *refs = refs
    else:
        x_ref, *refs = refs
    g_ref, wup_ref, cw_ref, wd_ref, fg_ref, o_ref, xn_scr, hbuf0, hbuf1, abuf0, abuf1, carry = refs
    sb = pl.program_id(1)
    nsub = x_ref.shape[1] // tm
    halves = lambda c: ((slice(0, FFN_CHUNK), c * FFN_CHUNK),
                        (slice(FFN_CHUNK, 2 * FFN_CHUNK), D_FF + c * FFN_CHUNK))

    def start(r):
        rows = slice(r * tm, (r + 1) * tm)
        x = x_ref[0, rows, :]
        if mix_out:
            x = (x + jnp.dot(u_ref[0, rows, :], wu_ref[...], preferred_element_type=F32)
                 + jnp.dot(a_ref[0, rows, :], wa_ref[...], preferred_element_type=F32))
        xn_scr[r] = _rms(x, g_ref[...]).astype(BF16)
        o_ref[0, rows, :] = x

    def up(r, c, hbuf):
        first = sb == 0 if r == 0 else False
        for lanes, col in halves(c):
            h = jnp.dot(xn_scr[r], wup_ref[:, col:col + FFN_CHUNK], preferred_element_type=F32)
            hbuf[0:FFN_HALO, lanes] = jnp.where(first, 0.0, carry[c, :, lanes])
            hbuf[FFN_HALO:FFN_HALO + tm, lanes] = h
            carry[c, :, lanes] = h[tm - FFN_HALO:, :]

    def gate(c, hbuf, abuf):
        cv = []
        for lanes, col in halves(c):
            w = cw_ref[:, col:col + FFN_CHUNK]
            v = w[FFN_K - 1:FFN_K, :] * hbuf[FFN_HALO:FFN_HALO + tm, lanes] + w[FFN_K:FFN_K + 1, :]
            for k in range(FFN_K - 1):
                off = FFN_HALO - (FFN_K - 1) + k
                v = v + w[k:k + 1, :] * hbuf[off:off + tm, lanes]
            cv.append(v)
        abuf[...] = (cv[0] * _sigmoid(cv[0]) * cv[1]).astype(BF16)

    def down(r, c, abuf):
        rows = slice(r * tm, (r + 1) * tm)
        o_ref[0, rows, :] += jnp.dot(abuf[...], wd_ref[c * FFN_CHUNK:(c + 1) * FFN_CHUNK, :],
                                     preferred_element_type=F32)
        if final_norm and c == FFN_NCHUNK - 1:
            o_ref[0, rows, :] = _rms(o_ref[0, rows, :], fg_ref[...])

    hb, ab = (hbuf0, hbuf1), (abuf0, abuf1)
    items = [(r, c) for r in range(nsub) for c in range(FFN_NCHUNK)]
    start(0)
    up(0, 0, hb[0])
    for i, (r, c) in enumerate(items):
        if i + 1 < len(items):
            rn, cn = items[i + 1]
            if cn == 0:
                start(rn)
            up(rn, cn, hb[(i + 1) % 2])
        if i:
            down(*items[i - 1], ab[(i - 1) % 2])
        gate(c, hb[i % 2], ab[i % 2])
    down(*items[-1], ab[(len(items) - 1) % 2])


def _ffn(x, mix, g, wup, cw, wd, fg, *, tm, nsub, final_norm):
    b, s, d = x.shape
    const = lambda a: pl.BlockSpec(a.shape, lambda i, j: (0,) * a.ndim, pipeline_mode=pl.Buffered(1))
    row = lambda w: pl.BlockSpec((1, nsub * tm, w), lambda i, j: (i, j, 0))
    ins, specs = [x], [row(d)]
    if mix is not None:
        u, a, wu, wa = mix
        ins += [u, a, wu, wa]
        specs += [row(u.shape[-1]), row(a.shape[-1]), const(wu), const(wa)]
    ins += [g, wup, cw, wd, fg]
    specs += [const(g), const(wup), const(cw), const(wd), const(fg)]
    return pl.pallas_call(
        functools.partial(_ffn_kernel, tm=tm, mix_out=mix is not None, final_norm=final_norm),
        grid=(b, s // (nsub * tm)),
        in_specs=specs,
        out_specs=row(d),
        out_shape=jax.ShapeDtypeStruct((b, s, d), F32),
        scratch_shapes=[pltpu.VMEM((nsub, tm, d), BF16),
                        pltpu.VMEM((tm + FFN_HALO, 2 * FFN_CHUNK), F32),
                        pltpu.VMEM((tm + FFN_HALO, 2 * FFN_CHUNK), F32),
                        pltpu.VMEM((tm, FFN_CHUNK), BF16),
                        pltpu.VMEM((tm, FFN_CHUNK), BF16),
                        pltpu.VMEM((FFN_NCHUNK, FFN_HALO, 2 * FFN_CHUNK), F32)],
        compiler_params=_cparams(("arbitrary", "arbitrary")),
        name="ffn_final" if final_norm else "ffn",
    )(*ins)


def _regroup(slabs, n_out, pieces):
    rows = slabs[0].shape[0]
    quarter = lax.broadcasted_iota(jnp.int32, (rows, LANES), 1) // SSM_PAIR_CH
    out = []
    for d in range(n_out):
        acc = None
        for dst_q, (src, src_q) in enumerate(pieces(d)):
            shift = ((dst_q - src_q) * SSM_PAIR_CH) % LANES
            piece = pltpu.roll(slabs[src], shift, axis=1) if shift else slabs[src]
            acc = piece if acc is None else jnp.where(quarter == dst_q, piece, acc)
        out.append(acc)
    return out


def _shift_rows(x, n, row):
    return jnp.where(row >= n, pltpu.roll(x, n, axis=0), 0.0)


def _mix1_kernel(x_ref, g_ref, win_ref, mb_ref, t_ref, mc_ref, lp_ref, d_ref, wglu_ref, bglu_ref,
                 o_ref, carry, u_slab, y_slab, *, rows, nlev):
    sb = pl.program_id(1)

    @pl.when(sb == 0)
    def _():
        carry[...] = jnp.zeros(carry.shape, F32)

    x = x_ref[0]
    xn = _rms(x, g_ref[...]).astype(BF16)
    u = jnp.dot(xn, win_ref[...], preferred_element_type=F32)
    nch = SSM_WIDTH // LANES
    for s in range(nch):
        u_slab[s] = u[:, s * LANES:(s + 1) * LANES]
    nslab = SSM_ROW // LANES
    u_slabs = [u_slab[i % nch, pl.ds(i // nch, rows, stride=SSM_L), :] for i in range(nslab)]
    a_slabs = _regroup(u_slabs, nslab,
                       lambda d: [(4 * (4 * (d % 2) + i) + (d // 2) // 4, (d // 2) % 4) for i in range(4)])
    row = lax.broadcasted_iota(jnp.int32, (rows, LANES), 0)
    first = row == 0
    y_slabs = []
    for p in range(SSM_PAIRS):
        a = jnp.concatenate(a_slabs[2 * p:2 * p + 2], axis=1)
        a16 = a.astype(BF16)
        lo = slice(p * SSM_TILE, p * SSM_TILE + LANES)
        hi = slice(p * SSM_TILE + LANES, (p + 1) * SSM_TILE)
        sloc = jnp.dot(a16, mb_ref[p], preferred_element_type=F32)
        sr, si = sloc[:, :LANES], sloc[:, LANES:]
        cr, ci = carry[0:1, lo], carry[0:1, hi]
        lr, li = lp_ref[0:1, lo], lp_ref[0:1, hi]
        sr = sr + jnp.where(first, lr * cr - li * ci, 0.0)
        si = si + jnp.where(first, lr * ci + li * cr, 0.0)
        for lev in range(nlev):
            n = 1 << lev
            lr, li = lp_ref[lev:lev + 1, lo], lp_ref[lev:lev + 1, hi]
            pr, pi = _shift_rows(sr, n, row), _shift_rows(si, n, row)
            sr, si = sr + (lr * pr - li * pi), si + (lr * pi + li * pr)
        xr = jnp.where(first, cr, pltpu.roll(sr, 1, axis=0))
        xi = jnp.where(first, ci, pltpu.roll(si, 1, axis=0))
        carry[0:1, lo] = sr[rows - 1:rows, :]
        carry[0:1, hi] = si[rows - 1:rows, :]
        xp = jnp.concatenate([xr, xi], axis=1).astype(BF16)
        y = (jnp.dot(a16, t_ref[p], preferred_element_type=F32)
             + jnp.dot(xp, mc_ref[p], preferred_element_type=F32)
             + d_ref[:, p * SSM_TILE:(p + 1) * SSM_TILE] * a)
        y = jax.nn.gelu(y, approximate=True)
        y_slabs += [y[:, :LANES], y[:, LANES:]]
    o_slabs = _regroup(y_slabs, nslab,
                       lambda s: [(2 * (4 * (s % 4) + j) + (s // 4) // 4, (s // 4) % 4) for j in range(4)])
    for i in range(nslab):
        y_slab[i % nch, pl.ds(i // nch, rows, stride=SSM_L), :] = o_slabs[i]
    y = jnp.concatenate([y_slab[s] for s in range(nch)], axis=1).astype(BF16)
    z = jnp.dot(y, wglu_ref[...], preferred_element_type=F32) + bglu_ref[...]
    o_ref[0] = x + z[:, :D_MODEL] * _sigmoid(z[:, D_MODEL:])


def _mix1(x, g, win, mb, tt, mc, lp, dd, wglu, bglu, *, rows):
    b, s, d = x.shape
    tok = rows * SSM_L
    nlev = int(math.log2(rows))
    assert 1 << nlev == rows and lp.shape[0] >= nlev
    const = lambda a: pl.BlockSpec(a.shape, lambda i, j: (0,) * a.ndim, pipeline_mode=pl.Buffered(1))
    return pl.pallas_call(
        functools.partial(_mix1_kernel, rows=rows, nlev=nlev),
        grid=(b, s // tok),
        in_specs=[pl.BlockSpec((1, tok, d), lambda i, j: (i, j, 0)),
                  const(g), const(win), const(mb), const(tt), const(mc), const(lp), const(dd),
                  const(wglu), const(bglu)],
        out_specs=pl.BlockSpec((1, tok, d), lambda i, j: (i, j, 0)),
        out_shape=jax.ShapeDtypeStruct((b, s, d), F32),
        scratch_shapes=[pltpu.VMEM((SUBLANES, SSM_ROW), F32),
                        pltpu.VMEM((SSM_WIDTH // LANES, tok, LANES), F32),
                        pltpu.VMEM((SSM_WIDTH // LANES, tok, LANES), F32)],
        compiler_params=_cparams(("arbitrary", "arbitrary")),
        name="mix1",
    )(x, g, win, mb, tt, mc, lp, dd, wglu, bglu)


def _ssm_tables(log_dt, a_re, a_im, b_re, b_im, c_re, c_im, d_skip, nlev):
    g, p, c, L = SSM_GROUPS, SSM_STATE, SSM_GROUP, SSM_L
    dt = jnp.exp(log_dt.astype(F32))[:, None]
    ar, ai = a_re.astype(F32), a_im.astype(F32)
    mag = jnp.exp(ar * dt)
    lb_re, lb_im = mag * jnp.cos(ai * dt), mag * jnp.sin(ai * dt)
    den = ar * ar + ai * ai
    nr, ni = lb_re - 1.0, lb_im
    f_re = (nr * ar + ni * ai) / den
    f_im = (ni * ar - nr * ai) / den
    br, bi = b_re.astype(F32), b_im.astype(F32)
    bb_re = f_re[..., None] * br - f_im[..., None] * bi
    bb_im = f_re[..., None] * bi + f_im[..., None] * br

    def lam_pow(n):
        n = jnp.asarray(n, F32)[:, None, None]
        m = jnp.exp(n * (ar * dt))
        return m * jnp.cos(n * (ai * dt)), m * jnp.sin(n * (ai * dt))

    cr, ci = c_re.astype(F32), c_im.astype(F32)
    pr, pi = lam_pow(jnp.arange(L + 1))
    lbr = pr[..., None] * bb_re[None] - pi[..., None] * bb_im[None]
    lbi = pr[..., None] * bb_im[None] + pi[..., None] * bb_re[None]
    kk = (jnp.einsum('gop,ngpc->ngoc', cr, lbr[:L]) - jnp.einsum('gop,ngpc->ngoc', ci, lbi[:L]))
    cat = jnp.concatenate
    q = SSM_PAIRS

    def pair_diag(a0, a1):
        z = jnp.zeros_like(a0)
        return cat([cat([a0, z], axis=2), cat([z, a1], axis=2)], axis=1)

    kt = kk.transpose(0, 1, 3, 2).reshape(L, q, 2, c, c)
    kbd = [pair_diag(kt[n, :, 0], kt[n, :, 1]) for n in range(L)]
    zero = jnp.zeros_like(kbd[0])
    tt = cat([cat([kbd[t2 - t1] if t2 >= t1 else zero for t2 in range(L)], axis=2) for t1 in range(L)], axis=1)
    def mb_rows(n):
        r = lbr[n].transpose(0, 2, 1).reshape(q, 2, c, p)
        i = lbi[n].transpose(0, 2, 1).reshape(q, 2, c, p)
        return cat([pair_diag(r[:, 0], r[:, 1]), pair_diag(i[:, 0], i[:, 1])], axis=2)
    mb = cat([mb_rows(L - 1 - t) for t in range(L)], axis=1)
    p1r, p1i = pr[1:], pi[1:]
    clr = cr[None] * p1r[:, :, None, :] - ci[None] * p1i[:, :, None, :]
    cli = cr[None] * p1i[:, :, None, :] + ci[None] * p1r[:, :, None, :]
    def mc_cols(t):
        r = clr[t].transpose(0, 2, 1).reshape(q, 2, p, c)
        i = -cli[t].transpose(0, 2, 1).reshape(q, 2, p, c)
        return cat([pair_diag(r[:, 0], r[:, 1]), pair_diag(i[:, 0], i[:, 1])], axis=1)
    mc = cat([mc_cols(t) for t in range(L)], axis=2)
    sr, si = lam_pow(L * (2 ** jnp.arange(nlev)))
    lp = jnp.stack([sr.reshape(nlev, SSM_PAIRS, 2 * p), si.reshape(nlev, SSM_PAIRS, 2 * p)], axis=2)
    lp = lp.reshape(nlev, SSM_ROW)
    lp = jnp.pad(lp, ((0, (-nlev) % SUBLANES), (0, 0)))
    dd = jnp.broadcast_to(d_skip.astype(F32).reshape(SSM_PAIRS, 1, SSM_PAIR_CH),
                          (SSM_PAIRS, L, SSM_PAIR_CH)).reshape(1, SSM_ROW)
    return mb.astype(BF16), tt.astype(BF16), mc.astype(BF16), lp, dd


def _head_pad_cols(nope, rope):
    k = nope.shape[0]
    parts = [nope]
    if rope is not None:
        parts.append(rope)
    used = sum(t.shape[-1] for t in parts)
    parts.append(jnp.zeros((k, N_HEADS, HEAD_PAD - used), nope.dtype))
    return jnp.concatenate(parts, axis=-1).reshape(k, QK_WIDTH)


def _rope_tables(s):
    inv = ROPE_BASE ** (-jnp.arange(ROPE_HALF, dtype=F32) / ROPE_HALF)
    ang = jnp.arange(s, dtype=F32)[:, None] * inv[None, :]
    cos, sin = jnp.cos(ang), jnp.sin(ang)
    gap = jnp.zeros((s, QK_NOPE_DIM - QK_ROPE_DIM), F32)
    tail = jnp.ones((s, HEAD_PAD - QK_NOPE_DIM - QK_ROPE_DIM), F32)
    return jnp.concatenate([sin, -sin, gap, cos, cos, tail], axis=1)


def _pick(total, want):
    t = min(total, want)
    assert total % t == 0
    return t


def kernel(x, l0_mix_norm, l0_w_in, l0_conv_w, l0_conv_b, l0_conv_ln_g, l0_conv_ln_b, l0_q_norm, l0_kv_norm, l0_w_uq, l0_w_ukv, l0_w_out, l0_ffn_norm, l0_w_up, l0_ffn_conv_w, l0_ffn_conv_b, l0_w_down, l1_mix_norm, l1_w_in, l1_log_dt, l1_a_re, l1_a_im, l1_b_re, l1_b_im, l1_c_re, l1_c_im, l1_d, l1_w_glu, l1_b_glu, l1_ffn_norm, l1_w_up, l1_ffn_conv_w, l1_ffn_conv_b, l1_w_down, final_norm):
    b, s, d = x.shape
    row2 = lambda v: v.reshape(1, -1).astype(F32)

    o = 2 * CONV_WIDTH + Q_LORA_RANK + KV_LORA_RANK
    w_kr = l0_w_in[:, o:]
    kr_blk = jnp.concatenate([jnp.zeros((d, QK_NOPE_DIM), F32), w_kr,
                              jnp.zeros((d, HEAD_PAD - QK_NOPE_DIM - QK_ROPE_DIM), F32)], axis=1)
    w_in0 = jnp.concatenate([l0_w_in[:, :o], kr_blk], axis=1).astype(BF16)
    scale = (QK_NOPE_DIM + QK_ROPE_DIM) ** -0.5 * math.log2(math.e)
    wq = (l0_w_uq * scale).reshape(Q_LORA_RANK, N_HEADS, QK_NOPE_DIM + QK_ROPE_DIM)
    wq_rot = jnp.concatenate([-wq[..., QK_NOPE_DIM + ROPE_HALF:], wq[..., QK_NOPE_DIM:QK_NOPE_DIM + ROPE_HALF]], axis=-1)
    wuq = jnp.concatenate([_head_pad_cols(wq[..., :QK_NOPE_DIM], wq[..., QK_NOPE_DIM:]),
                           _head_pad_cols(jnp.zeros_like(wq[..., :QK_NOPE_DIM]), wq_rot)], axis=1).astype(BF16)
    wkv = l0_w_ukv.reshape(KV_LORA_RANK, N_HEADS, QK_NOPE_DIM + V_HEAD_DIM)
    wukv = jnp.concatenate([_head_pad_cols(wkv[..., :QK_NOPE_DIM], None),
                            wkv[..., QK_NOPE_DIM:].reshape(KV_LORA_RANK, ATTN_WIDTH)], axis=1).astype(BF16)
    cw0 = jnp.pad(l0_conv_w.astype(F32), ((0, CONV_HALO - CONV_K), (0, 0)))
    rope = _rope_tables(s)

    t_attn = _pick(s, 512)
    u0, q, k, vt = _pre0(x, row2(l0_mix_norm), w_in0, rope, cw0, row2(l0_conv_b),
                         row2(l0_conv_ln_g), row2(l0_conv_ln_b), row2(l0_q_norm), row2(l0_kv_norm),
                         wuq, wukv, tm=t_attn)
    attn = _attention(q, k, vt, t=t_attn)
    w_out = l0_w_out.astype(BF16)
    mix0 = (u0, attn, w_out[:CONV_WIDTH], w_out[CONV_WIDTH:])

    tm_ffn = _pick(s, 512)

    def ffn(xx, mix, g, w_up, cw, cb, w_down, final):
        taps = jnp.concatenate([cw.astype(F32), cb.astype(F32)[None],
                                jnp.zeros((SUBLANES - FFN_K - 1, 2 * D_FF), F32)], axis=0)
        return _ffn(xx, mix, row2(g), w_up.astype(BF16), taps, w_down.astype(BF16),
                    row2(final_norm), tm=tm_ffn, nsub=1, final_norm=final)

    x2 = ffn(x, mix0, l0_ffn_norm, l0_w_up, l0_ffn_conv_w, l0_ffn_conv_b, l0_w_down, False)

    rows = _pick(s // SSM_L, 128)
    nlev = int(math.log2(rows))
    mb, tt, mc, lp, dd = _ssm_tables(l1_log_dt, l1_a_re, l1_a_im, l1_b_re, l1_b_im,
                                     l1_c_re, l1_c_im, l1_d, nlev)
    x3 = _mix1(x2, row2(l1_mix_norm), l1_w_in.astype(BF16), mb, tt, mc, lp, dd,
               l1_w_glu.astype(BF16), row2(l1_b_glu), rows=rows)
    return ffn(x3, None, l1_ffn_norm, l1_w_up, l1_ffn_conv_w, l1_ffn_conv_b, l1_w_down, True)
```

```python
import functools
import math

import jax
import jax.numpy as jnp
from jax import lax
from jax.experimental import pallas as pl
from jax.experimental.pallas import tpu as pltpu

F32 = jnp.float32
BF16 = jnp.bfloat16

D_MODEL = 1024
EPS = 1e-6
LN_EPS = 1e-5
CONV_WIDTH = 512
CONV_K = 31
N_HEADS = 8
QK_NOPE_DIM = 64
QK_ROPE_DIM = 32
ROPE_HALF = QK_ROPE_DIM // 2
V_HEAD_DIM = 64
Q_LORA_RANK = 256
KV_LORA_RANK = 128
ROPE_BASE = 10000.0
ATTN_WIDTH = N_HEADS * V_HEAD_DIM
SSM_WIDTH = 512
SSM_GROUP = 16
SSM_GROUPS = SSM_WIDTH // SSM_GROUP
SSM_STATE = 64
D_FF = 2816
FFN_K = 3

LANES = 128
SUBLANES = 8
V_ROWS = V_HEAD_DIM + 16
HEAD_PAD = LANES
QK_WIDTH = N_HEADS * HEAD_PAD
CONV_HALO = 32
CONV_ROWS = 64
FFN_HALO = SUBLANES
FFN_CHUNK = 256
FFN_NCHUNK = D_FF // FFN_CHUNK
SSM_L = 8
SSM_PAIRS = SSM_GROUPS // 2
SSM_PAIR_CH = 2 * SSM_GROUP
SSM_TILE = SSM_L * SSM_PAIR_CH
SSM_ROW = SSM_L * SSM_WIDTH
VMEM_LIMIT = 56 * 1024 * 1024


def _cparams(sem):
    return pltpu.CompilerParams(dimension_semantics=sem, vmem_limit_bytes=VMEM_LIMIT)


def _rms(x, g):
    return x * lax.rsqrt(jnp.mean(x * x, axis=-1, keepdims=True) + EPS) * g


def _sigmoid(x):
    return 1.0 / (1.0 + jnp.exp(-x))


def _rope128(t, cos, sin_up, sin_dn):
    return (t * cos + pltpu.roll(t, ROPE_HALF, axis=1) * sin_up
            + pltpu.roll(t, LANES - ROPE_HALF, axis=1) * sin_dn)


def _rope_unpack(packed):
    lane = lax.broadcasted_iota(jnp.int32, packed.shape, 1)
    r1, r2 = QK_NOPE_DIM, QK_NOPE_DIM + ROPE_HALF
    cos = jnp.where(lane < r1, 1.0, packed)
    up = pltpu.roll(packed, r2, axis=1)
    dn = pltpu.roll(packed, r1 - ROPE_HALF, axis=1)
    sin_up = jnp.where(lane >= r2, jnp.where(lane < r2 + ROPE_HALF, up, 0.0), 0.0)
    sin_dn = jnp.where(lane >= r1, jnp.where(lane < r2, dn, 0.0), 0.0)
    return cos, sin_up, sin_dn


def _pre0_kernel(x_ref, g_ref, w_in_ref, rope_ref, cw_ref, cb_ref, lng_ref, lnb_ref,
                 qn_ref, kvn_ref, wuq_ref, wukv_ref, wvt_ref,
                 u_ref, q_ref, k_ref, vt_ref, ubuf, ushift, h_scr, q_scr, kv_scr, vt_scr, *, tm):
    sb = pl.program_id(1)

    @pl.when(sb == 0)
    def _():
        ubuf[0:CONV_HALO, :] = jnp.zeros((CONV_HALO, CONV_WIDTH), F32)

    hm = tm // 2
    n_in = w_in_ref.shape[1] // (2 * LANES)
    base = CONV_HALO - (CONV_K - 1)

    def in_proj(half):
        xn = _rms(x_ref[0, half * hm:(half + 1) * hm, :], g_ref[...]).astype(BF16)

        def piece(n):
            cols = slice(n * 2 * LANES, (n + 1) * 2 * LANES)
            h_scr[half, :, cols] = jnp.dot(xn, w_in_ref[:, cols], preferred_element_type=F32)
        return [functools.partial(piece, n) for n in range(n_in)]

    def qkv_proj(half):
        h = h_scr.at[half]
        o = 2 * CONV_WIDTH

        def q_piece():
            cq = _rms(h[:, o:o + Q_LORA_RANK], qn_ref[...]).astype(BF16)
            q_scr[...] = jnp.dot(cq, wuq_ref[...], preferred_element_type=F32)

        def kv_piece():
            o2 = o + Q_LORA_RANK
            ckv = _rms(h[:, o2:o2 + KV_LORA_RANK], kvn_ref[...]).astype(BF16)
            kv_scr[...] = jnp.dot(ckv, wukv_ref[...], preferred_element_type=F32)
            vt_scr[...] = lax.dot_general(wvt_ref[...], ckv, (((1,), (1,)), ((), ())), preferred_element_type=F32)
        return [q_piece, kv_piece]

    def conv(half, between):
        h = h_scr.at[half]
        t0 = half * hm
        ubuf[CONV_HALO + t0:CONV_HALO + t0 + hm, :] = h[:, :CONV_WIDTH] * _sigmoid(h[:, CONV_WIDTH:2 * CONV_WIDTH])
        span = hm + CONV_HALO - SUBLANES
        for r in range(1, SUBLANES):
            ushift[r - 1, t0:t0 + span, :] = ubuf[t0 + r:t0 + r + span, :]
        chunks = range(t0, t0 + hm, CONV_ROWS)
        for i, r0 in enumerate(chunks):
            acc = jnp.broadcast_to(cb_ref[...], (CONV_ROWS, CONV_WIDTH))
            for o in range(base, base + CONV_K):
                r, al = o % SUBLANES, o - o % SUBLANES + r0
                tap = ushift[r - 1, al:al + CONV_ROWS, :] if r else ubuf[al:al + CONV_ROWS, :]
                acc = acc + cw_ref[o - base:o - base + 1, :] * tap
            mu = jnp.mean(acc, axis=-1, keepdims=True)
            cen = acc - mu
            var = jnp.mean(cen * cen, axis=-1, keepdims=True)
            y = cen * lax.rsqrt(var + LN_EPS) * lng_ref[...] + lnb_ref[...]
            u_ref[0, r0:r0 + CONV_ROWS, :] = (y * _sigmoid(y)).astype(BF16)
            for thunk in between[i * len(between) // len(chunks):(i + 1) * len(between) // len(chunks)]:
                thunk()

    def rotary_out(half):
        rows = slice(half * hm, (half + 1) * hm)
        cos, sin_up, sin_dn = _rope_unpack(rope_ref[rows, :])
        o = 2 * CONV_WIDTH + Q_LORA_RANK + KV_LORA_RANK
        kr = _rope128(h_scr[half, :, o:o + LANES], cos, sin_up, sin_dn)
        sin = sin_up - sin_dn
        for hd in range(N_HEADS):
            sl = slice(hd * HEAD_PAD, (hd + 1) * HEAD_PAD)
            rot = slice(QK_WIDTH + hd * HEAD_PAD, QK_WIDTH + (hd + 1) * HEAD_PAD)
            q_ref[0, rows, sl] = (q_scr[:, sl] * cos + q_scr[:, rot] * sin).astype(BF16)
            k_ref[0, rows, sl] = (kv_scr[:, sl] + kr).astype(BF16)
        vt = vt_scr[...]
        ones = jnp.ones((V_ROWS - V_HEAD_DIM, hm), F32)
        parts = [blk for hd in range(N_HEADS) for blk in (vt[hd * V_HEAD_DIM:(hd + 1) * V_HEAD_DIM], ones)]
        vt_ref[0, 0, :, rows] = jnp.concatenate(parts, axis=0).astype(BF16)

    for thunk in in_proj(0):
        thunk()
    conv(0, in_proj(1))
    conv(1, qkv_proj(0))
    rotary_out(0)
    for thunk in qkv_proj(1):
        thunk()
    rotary_out(1)
    ubuf[0:CONV_HALO, :] = ubuf[tm:tm + CONV_HALO, :]


def _pre0(x, g, w_in, rope, cw, cb, lng, lnb, qn, kvn, wuq, wukv, wvt, *, tm):
    b, s, d = x.shape
    const = lambda shape: pl.BlockSpec(shape, lambda i, j: (0,) * len(shape))
    row = lambda w: pl.BlockSpec((1, tm, w), lambda i, j: (i, j, 0))
    return pl.pallas_call(
        functools.partial(_pre0_kernel, tm=tm),
        grid=(b, s // tm),
        in_specs=[row(d), const(g.shape), const(w_in.shape),
                  pl.BlockSpec((tm, LANES), lambda i, j: (j, 0)),
                  const(cw.shape), const(cb.shape), const(lng.shape), const(lnb.shape),
                  const(qn.shape), const(kvn.shape), const(wuq.shape), const(wukv.shape), const(wvt.shape)],
        out_specs=[row(CONV_WIDTH), row(QK_WIDTH), row(QK_WIDTH),
                   pl.BlockSpec((1, 1, N_HEADS * V_ROWS, tm), lambda i, j: (i, j, 0, 0))],
        out_shape=[jax.ShapeDtypeStruct((b, s, CONV_WIDTH), BF16),
                   jax.ShapeDtypeStruct((b, s, QK_WIDTH), BF16),
                   jax.ShapeDtypeStruct((b, s, QK_WIDTH), BF16),
                   jax.ShapeDtypeStruct((b, s // tm, N_HEADS * V_ROWS, tm), BF16)],
        scratch_shapes=[pltpu.VMEM((tm + CONV_HALO, CONV_WIDTH), F32),
                        pltpu.VMEM((SUBLANES - 1, tm + CONV_HALO, CONV_WIDTH), F32),
                        pltpu.VMEM((2, tm // 2, w_in.shape[1]), F32),
                        pltpu.VMEM((tm // 2, wuq.shape[1]), F32),
                        pltpu.VMEM((tm // 2, wukv.shape[1]), F32),
                        pltpu.VMEM((wvt.shape[0], tm // 2), F32)],
        compiler_params=_cparams(("arbitrary", "arbitrary")),
        name="pre0",
    )(x, g, w_in, rope, cw, cb, lng, lnb, qn, kvn, wuq, wukv, wvt)


NEG_BIG = -1e30
ATTN_HEADS = 4
KV_UNROLL = 4


def _attn_kernel(q_ref, k_ref, vt_ref, o_ref, *s_refs, t):
    qi = pl.program_id(2)
    nh = len(s_refs)

    def scores(j, hh):
        lanes = slice(hh * HEAD_PAD, (hh + 1) * HEAD_PAD)
        k = k_ref[0, pl.ds(pl.multiple_of(j * t, t), t), lanes]
        s_refs[hh][...] = lax.dot_general(k, q_ref[0, :, lanes], (((1,), (1,)), ((), ())),
                                          preferred_element_type=F32)

    def update(j, hh, state, masked):
        m, acc = state
        st = s_refs[hh][...]
        if masked:
            key = lax.broadcasted_iota(jnp.int32, (t, t), 0)
            qry = lax.broadcasted_iota(jnp.int32, (t, t), 1)
            st = jnp.where(key <= qry, st, NEG_BIG)
        m_new = jnp.maximum(m, jnp.max(st, axis=0, keepdims=True))
        alpha = jnp.exp2(m - m_new)
        p = jnp.exp2(st - m_new).astype(BF16)
        vt = vt_ref[0, j, hh * V_ROWS:(hh + 1) * V_ROWS, :]
        acc = alpha * acc + jnp.dot(vt, p, preferred_element_type=F32)
        return m_new, acc

    def block(j, carry, masked):
        out = ()
        for hh in range(nh):
            if hh + 1 < nh:
                scores(j, hh + 1)
            elif not masked:
                scores(j + 1, 0)
            out += update(j, hh, carry[2 * hh:2 * hh + 2], masked)
        return out

    body = functools.partial(block, masked=False)

    def unrolled(i, c):
        for u in range(KV_UNROLL):
            c = body(KV_UNROLL * i + u, c)
        return c

    init = (jnp.full((1, t), NEG_BIG, F32), jnp.zeros((V_ROWS, t), F32)) * nh
    scores(0, 0)
    carry = lax.fori_loop(0, qi // KV_UNROLL, unrolled, init)
    carry = lax.fori_loop(qi // KV_UNROLL * KV_UNROLL, qi, body, carry)
    carry = block(qi, carry, True)
    d = V_HEAD_DIM
    ot = jnp.concatenate([carry[2 * hh + 1][:d] / carry[2 * hh + 1][d:d + 1] for hh in range(nh)], axis=0)
    o_ref[0] = ot.T.astype(o_ref.dtype)


def _attention(q, k, vt, *, t):
    b, s, _ = q.shape
    assert vt.shape == (b, s // t, N_HEADS * V_ROWS, t)
    nh = ATTN_HEADS
    return pl.pallas_call(
        functools.partial(_attn_kernel, t=t),
        grid=(b, N_HEADS // nh, s // t),
        in_specs=[pl.BlockSpec((1, t, nh * HEAD_PAD), lambda i, h, j: (i, j, h)),
                  pl.BlockSpec((1, s, nh * HEAD_PAD), lambda i, h, j: (i, 0, h)),
                  pl.BlockSpec((1, s // t, nh * V_ROWS, t), lambda i, h, j: (i, 0, h, 0))],
        out_specs=pl.BlockSpec((1, t, nh * V_HEAD_DIM), lambda i, h, j: (i, j, h)),
        out_shape=jax.ShapeDtypeStruct((b, s, ATTN_WIDTH), BF16),
        scratch_shapes=[pltpu.VMEM((t, t), F32)] * nh,
        compiler_params=_cparams(("arbitrary", "arbitrary", "arbitrary")),
        name="attn",
    )(q, k, vt)


def _ffn_kernel(*refs, tm, mix_out, final_norm):
    if mix_out:
        x_ref, u_ref, a_ref, wu_ref, wa_ref, *refs = refs
    else:
        x_ref, *refs = refs
    g_ref, wup_ref, cw_ref, wd_ref, fg_ref, o_ref, xn_scr, hbuf0, hbuf1, abuf0, abuf1, carry = refs
    sb = pl.program_id(1)
    nsub = x_ref.shape[1] // tm
    halves = lambda c: ((slice(0, FFN_CHUNK), c * FFN_CHUNK),
                        (slice(FFN_CHUNK, 2 * FFN_CHUNK), D_FF + c * FFN_CHUNK))

    def start(r):
        rows = slice(r * tm, (r + 1) * tm)
        x = x_ref[0, rows, :]
        if mix_out:
            x = (x + jnp.dot(u_ref[0, rows, :], wu_ref[...], preferred_element_type=F32)
                 + jnp.dot(a_ref[0, rows, :], wa_ref[...], preferred_element_type=F32))
        xn_scr[r] = _rms(x, g_ref[...]).astype(BF16)
        o_ref[0, rows, :] = x

    def up(r, c, hbuf):
        first = sb == 0 if r == 0 else False
        for lanes, col in halves(c):
            h = jnp.dot(xn_scr[r], wup_ref[:, col:col + FFN_CHUNK], preferred_element_type=F32)
            hbuf[0:FFN_HALO, lanes] = jnp.where(first, 0.0, carry[c, :, lanes])
            hbuf[FFN_HALO:FFN_HALO + tm, lanes] = h
            carry[c, :, lanes] = h[tm - FFN_HALO:, :]

    def gate(c, hbuf, abuf):
        cv = []
        for lanes, col in halves(c):
            w = cw_ref[:, col:col + FFN_CHUNK]
            v = w[FFN_K - 1:FFN_K, :] * hbuf[FFN_HALO:FFN_HALO + tm, lanes] + w[FFN_K:FFN_K + 1, :]
            for k in range(FFN_K - 1):
                off = FFN_HALO - (FFN_K - 1) + k
                v = v + w[k:k + 1, :] * hbuf[off:off + tm, lanes]
            cv.append(v)
        abuf[...] = (cv[0] * _sigmoid(cv[0]) * cv[1]).astype(BF16)

    def down(r, c, abuf):
        rows = slice(r * tm, (r + 1) * tm)
        o_ref[0, rows, :] += jnp.dot(abuf[...], wd_ref[c * FFN_CHUNK:(c + 1) * FFN_CHUNK, :],
                                     preferred_element_type=F32)
        if final_norm and c == FFN_NCHUNK - 1:
            o_ref[0, rows, :] = _rms(o_ref[0, rows, :], fg_ref[...])

    hb, ab = (hbuf0, hbuf1), (abuf0, abuf1)
    items = [(r, c) for r in range(nsub) for c in range(FFN_NCHUNK)]
    start(0)
    up(0, 0, hb[0])
    for i, (r, c) in enumerate(items):
        if i + 1 < len(items):
            rn, cn = items[i + 1]
            if cn == 0:
                start(rn)
            up(rn, cn, hb[(i + 1) % 2])
        if i:
            down(*items[i - 1], ab[(i - 1) % 2])
        gate(c, hb[i % 2], ab[i % 2])
    down(*items[-1], ab[(len(items) - 1) % 2])


def _ffn(x, mix, g, wup, cw, wd, fg, *, tm, nsub, final_norm):
    b, s, d = x.shape
    const = lambda a: pl.BlockSpec(a.shape, lambda i, j: (0,) * a.ndim, pipeline_mode=pl.Buffered(1))
    row = lambda w: pl.BlockSpec((1, nsub * tm, w), lambda i, j: (i, j, 0))
    ins, specs = [x], [row(d)]
    if mix is not None:
        u, a, wu, wa = mix
        ins += [u, a, wu, wa]
        specs += [row(u.shape[-1]), row(a.shape[-1]), const(wu), const(wa)]
    ins += [g, wup, cw, wd, fg]
    specs += [const(g), const(wup), const(cw), const(wd), const(fg)]
    return pl.pallas_call(
        functools.partial(_ffn_kernel, tm=tm, mix_out=mix is not None, final_norm=final_norm),
        grid=(b, s // (nsub * tm)),
        in_specs=specs,
        out_specs=row(d),
        out_shape=jax.ShapeDtypeStruct((b, s, d), F32),
        scratch_shapes=[pltpu.VMEM((nsub, tm, d), BF16),
                        pltpu.VMEM((tm + FFN_HALO, 2 * FFN_CHUNK), F32),
                        pltpu.VMEM((tm + FFN_HALO, 2 * FFN_CHUNK), F32),
                        pltpu.VMEM((tm, FFN_CHUNK), BF16),
                        pltpu.VMEM((tm, FFN_CHUNK), BF16),
                        pltpu.VMEM((FFN_NCHUNK, FFN_HALO, 2 * FFN_CHUNK), F32)],
        compiler_params=_cparams(("arbitrary", "arbitrary")),
        name="ffn_final" if final_norm else "ffn",
    )(*ins)


def _regroup(slabs, n_out, pieces):
    rows = slabs[0].shape[0]
    quarter = lax.broadcasted_iota(jnp.int32, (rows, LANES), 1) // SSM_PAIR_CH
    out = []
    for d in range(n_out):
        acc = None
        for dst_q, (src, src_q) in enumerate(pieces(d)):
            shift = ((dst_q - src_q) * SSM_PAIR_CH) % LANES
            piece = pltpu.roll(slabs[src], shift, axis=1) if shift else slabs[src]
            acc = piece if acc is None else jnp.where(quarter == dst_q, piece, acc)
        out.append(acc)
    return out


def _shift_rows(x, n, row):
    return jnp.where(row >= n, pltpu.roll(x, n, axis=0), 0.0)


def _mix1_kernel(x_ref, g_ref, win_ref, mb_ref, t_ref, mc_ref, lp_ref, d_ref, wglu_ref, bglu_ref,
                 o_ref, carry, u_slab, y_slab, *, rows, nlev):
    sb = pl.program_id(1)

    @pl.when(sb == 0)
    def _():
        carry[...] = jnp.zeros(carry.shape, F32)

    x = x_ref[0]
    xn = _rms(x, g_ref[...]).astype(BF16)
    u = jnp.dot(xn, win_ref[...], preferred_element_type=F32)
    nch = SSM_WIDTH // LANES
    for s in range(nch):
        u_slab[s] = u[:, s * LANES:(s + 1) * LANES]
    nslab = SSM_ROW // LANES
    u_slabs = [u_slab[i % nch, pl.ds(i // nch, rows, stride=SSM_L), :] for i in range(nslab)]
    a_slabs = _regroup(u_slabs, nslab,
                       lambda d: [(4 * (4 * (d % 2) + i) + (d // 2) // 4, (d // 2) % 4) for i in range(4)])
    row = lax.broadcasted_iota(jnp.int32, (rows, LANES), 0)
    first = row == 0
    y_slabs = []
    for p in range(SSM_PAIRS):
        a = jnp.concatenate(a_slabs[2 * p:2 * p + 2], axis=1)
        a16 = a.astype(BF16)
        lo = slice(p * SSM_TILE, p * SSM_TILE + LANES)
        hi = slice(p * SSM_TILE + LANES, (p + 1) * SSM_TILE)
        sloc = jnp.dot(a16, mb_ref[p], preferred_element_type=F32)
        sr, si = sloc[:, :LANES], sloc[:, LANES:]
        cr, ci = carry[0:1, lo], carry[0:1, hi]
        lr, li = lp_ref[0:1, lo], lp_ref[0:1, hi]
        sr = sr + jnp.where(first, lr * cr - li * ci, 0.0)
        si = si + jnp.where(first, lr * ci + li * cr, 0.0)
        for lev in range(nlev):
            n = 1 << lev
            lr, li = lp_ref[lev:lev + 1, lo], lp_ref[lev:lev + 1, hi]
            pr, pi = _shift_rows(sr, n, row), _shift_rows(si, n, row)
            sr, si = sr + (lr * pr - li * pi), si + (lr * pi + li * pr)
        xr = jnp.where(first, cr, pltpu.roll(sr, 1, axis=0))
        xi = jnp.where(first, ci, pltpu.roll(si, 1, axis=0))
        carry[0:1, lo] = sr[rows - 1:rows, :]
        carry[0:1, hi] = si[rows - 1:rows, :]
        xp = jnp.concatenate([xr, xi], axis=1).astype(BF16)
        y = (jnp.dot(a16, t_ref[p], preferred_element_type=F32)
             + jnp.dot(xp, mc_ref[p], preferred_element_type=F32)
             + d_ref[:, p * SSM_TILE:(p + 1) * SSM_TILE] * a)
        y = jax.nn.gelu(y, approximate=True)
        y_slabs += [y[:, :LANES], y[:, LANES:]]
    o_slabs = _regroup(y_slabs, nslab,
                       lambda s: [(2 * (4 * (s % 4) + j) + (s // 4) // 4, (s // 4) % 4) for j in range(4)])
    for i in range(nslab):
        y_slab[i % nch, pl.ds(i // nch, rows, stride=SSM_L), :] = o_slabs[i]
    y = jnp.concatenate([y_slab[s] for s in range(nch)], axis=1).astype(BF16)
    z = jnp.dot(y, wglu_ref[...], preferred_element_type=F32) + bglu_ref[...]
    o_ref[0] = x + z[:, :D_MODEL] * _sigmoid(z[:, D_MODEL:])


def _mix1(x, g, win, mb, tt, mc, lp, dd, wglu, bglu, *, rows):
    b, s, d = x.shape
    tok = rows * SSM_L
    nlev = int(math.log2(rows))
    assert 1 << nlev == rows and lp.shape[0] >= nlev
    const = lambda a: pl.BlockSpec(a.shape, lambda i, j: (0,) * a.ndim, pipeline_mode=pl.Buffered(1))
    return pl.pallas_call(
        functools.partial(_mix1_kernel, rows=rows, nlev=nlev),
        grid=(b, s // tok),
        in_specs=[pl.BlockSpec((1, tok, d), lambda i, j: (i, j, 0)),
                  const(g), const(win), const(mb), const(tt), const(mc), const(lp), const(dd),
                  const(wglu), const(bglu)],
        out_specs=pl.BlockSpec((1, tok, d), lambda i, j: (i, j, 0)),
        out_shape=jax.ShapeDtypeStruct((b, s, d), F32),
        scratch_shapes=[pltpu.VMEM((SUBLANES, SSM_ROW), F32),
                        pltpu.VMEM((SSM_WIDTH // LANES, tok, LANES), F32),
                        pltpu.VMEM((SSM_WIDTH // LANES, tok, LANES), F32)],
        compiler_params=_cparams(("arbitrary", "arbitrary")),
        name="mix1",
    )(x, g, win, mb, tt, mc, lp, dd, wglu, bglu)


def _ssm_tables(log_dt, a_re, a_im, b_re, b_im, c_re, c_im, d_skip, nlev):
    g, p, c, L = SSM_GROUPS, SSM_STATE, SSM_GROUP, SSM_L
    dt = jnp.exp(log_dt.astype(F32))[:, None]
    ar, ai = a_re.astype(F32), a_im.astype(F32)
    mag = jnp.exp(ar * dt)
    lb_re, lb_im = mag * jnp.cos(ai * dt), mag * jnp.sin(ai * dt)
    den = ar * ar + ai * ai
    nr, ni = lb_re - 1.0, lb_im
    f_re = (nr * ar + ni * ai) / den
    f_im = (ni * ar - nr * ai) / den
    br, bi = b_re.astype(F32), b_im.astype(F32)
    bb_re = f_re[..., None] * br - f_im[..., None] * bi
    bb_im = f_re[..., None] * bi + f_im[..., None] * br

    def lam_pow(n):
        n = jnp.asarray(n, F32)[:, None, None]
        m = jnp.exp(n * (ar * dt))
        return m * jnp.cos(n * (ai * dt)), m * jnp.sin(n * (ai * dt))

    cr, ci = c_re.astype(F32), c_im.astype(F32)
    pr, pi = lam_pow(jnp.arange(L + 1))
    lbr = pr[..., None] * bb_re[None] - pi[..., None] * bb_im[None]
    lbi = pr[..., None] * bb_im[None] + pi[..., None] * bb_re[None]
    kk = (jnp.einsum('gop,ngpc->ngoc', cr, lbr[:L]) - jnp.einsum('gop,ngpc->ngoc', ci, lbi[:L]))
    cat = jnp.concatenate
    q = SSM_PAIRS

    def pair_diag(a0, a1):
        z = jnp.zeros_like(a0)
        return cat([cat([a0, z], axis=2), cat([z, a1], axis=2)], axis=1)

    kt = kk.transpose(0, 1, 3, 2).reshape(L, q, 2, c, c)
    kbd = [pair_diag(kt[n, :, 0], kt[n, :, 1]) for n in range(L)]
    zero = jnp.zeros_like(kbd[0])
    tt = cat([cat([kbd[t2 - t1] if t2 >= t1 else zero for t2 in range(L)], axis=2) for t1 in range(L)], axis=1)
    def mb_rows(n):
        r = lbr[n].transpose(0, 2, 1).reshape(q, 2, c, p)
        i = lbi[n].transpose(0, 2, 1).reshape(q, 2, c, p)
        return cat([pair_diag(r[:, 0], r[:, 1]), pair_diag(i[:, 0], i[:, 1])], axis=2)
    mb = cat([mb_rows(L - 1 - t) for t in range(L)], axis=1)
    p1r, p1i = pr[1:], pi[1:]
    clr = cr[None] * p1r[:, :, None, :] - ci[None] * p1i[:, :, None, :]
    cli = cr[None] * p1i[:, :, None, :] + ci[None] * p1r[:, :, None, :]
    def mc_cols(t):
        r = clr[t].transpose(0, 2, 1).reshape(q, 2, p, c)
        i = -cli[t].transpose(0, 2, 1).reshape(q, 2, p, c)
        return cat([pair_diag(r[:, 0], r[:, 1]), pair_diag(i[:, 0], i[:, 1])], axis=1)
    mc = cat([mc_cols(t) for t in range(L)], axis=2)
    sr, si = lam_pow(L * (2 ** jnp.arange(nlev)))
    lp = jnp.stack([sr.reshape(nlev, SSM_PAIRS, 2 * p), si.reshape(nlev, SSM_PAIRS, 2 * p)], axis=2)
    lp = lp.reshape(nlev, SSM_ROW)
    lp = jnp.pad(lp, ((0, (-nlev) % SUBLANES), (0, 0)))
    dd = jnp.broadcast_to(d_skip.astype(F32).reshape(SSM_PAIRS, 1, SSM_PAIR_CH),
                          (SSM_PAIRS, L, SSM_PAIR_CH)).reshape(1, SSM_ROW)
    return mb.astype(BF16), tt.astype(BF16), mc.astype(BF16), lp, dd


def _head_pad_cols(nope, rope):
    k = nope.shape[0]
    parts = [nope]
    if rope is not None:
        parts.append(rope)
    used = sum(t.shape[-1] for t in parts)
    parts.append(jnp.zeros((k, N_HEADS, HEAD_PAD - used), nope.dtype))
    return jnp.concatenate(parts, axis=-1).reshape(k, QK_WIDTH)


def _rope_tables(s):
    inv = ROPE_BASE ** (-jnp.arange(ROPE_HALF, dtype=F32) / ROPE_HALF)
    ang = jnp.arange(s, dtype=F32)[:, None] * inv[None, :]
    cos, sin = jnp.cos(ang), jnp.sin(ang)
    gap = jnp.zeros((s, QK_NOPE_DIM - QK_ROPE_DIM), F32)
    tail = jnp.ones((s, HEAD_PAD - QK_NOPE_DIM - QK_ROPE_DIM), F32)
    return jnp.concatenate([sin, -sin, gap, cos, cos, tail], axis=1)


def _pick(total, want):
    t = min(total, want)
    assert total % t == 0
    return t


def kernel(x, l0_mix_norm, l0_w_in, l0_conv_w, l0_conv_b, l0_conv_ln_g, l0_conv_ln_b, l0_q_norm, l0_kv_norm, l0_w_uq, l0_w_ukv, l0_w_out, l0_ffn_norm, l0_w_up, l0_ffn_conv_w, l0_ffn_conv_b, l0_w_down, l1_mix_norm, l1_w_in, l1_log_dt, l1_a_re, l1_a_im, l1_b_re, l1_b_im, l1_c_re, l1_c_im, l1_d, l1_w_glu, l1_b_glu, l1_ffn_norm, l1_w_up, l1_ffn_conv_w, l1_ffn_conv_b, l1_w_down, final_norm):
    b, s, d = x.shape
    row2 = lambda v: v.reshape(1, -1).astype(F32)

    o = 2 * CONV_WIDTH + Q_LORA_RANK + KV_LORA_RANK
    w_kr = l0_w_in[:, o:]
    kr_blk = jnp.concatenate([jnp.zeros((d, QK_NOPE_DIM), F32), w_kr,
                              jnp.zeros((d, HEAD_PAD - QK_NOPE_DIM - QK_ROPE_DIM), F32)], axis=1)
    w_in0 = jnp.concatenate([l0_w_in[:, :o], kr_blk], axis=1).astype(BF16)
    scale = (QK_NOPE_DIM + QK_ROPE_DIM) ** -0.5 * math.log2(math.e)
    wq = (l0_w_uq * scale).reshape(Q_LORA_RANK, N_HEADS, QK_NOPE_DIM + QK_ROPE_DIM)
    wq_rot = jnp.concatenate([-wq[..., QK_NOPE_DIM + ROPE_HALF:], wq[..., QK_NOPE_DIM:QK_NOPE_DIM + ROPE_HALF]], axis=-1)
    wuq = jnp.concatenate([_head_pad_cols(wq[..., :QK_NOPE_DIM], wq[..., QK_NOPE_DIM:]),
                           _head_pad_cols(jnp.zeros_like(wq[..., :QK_NOPE_DIM]), wq_rot)], axis=1).astype(BF16)
    wkv = l0_w_ukv.reshape(KV_LORA_RANK, N_HEADS, QK_NOPE_DIM + V_HEAD_DIM)
    wukv = _head_pad_cols(wkv[..., :QK_NOPE_DIM], None).astype(BF16)
    wvt = wkv[..., QK_NOPE_DIM:].reshape(KV_LORA_RANK, ATTN_WIDTH).T.astype(BF16)
    cw0 = jnp.pad(l0_conv_w.astype(F32), ((0, CONV_HALO - CONV_K), (0, 0)))
    rope = _rope_tables(s)

    t_attn = _pick(s, 512)
    u0, q, k, vt = _pre0(x, row2(l0_mix_norm), w_in0, rope, cw0, row2(l0_conv_b),
                         row2(l0_conv_ln_g), row2(l0_conv_ln_b), row2(l0_q_norm), row2(l0_kv_norm),
                         wuq, wukv, wvt, tm=t_attn)
    attn = _attention(q, k, vt, t=t_attn)
    w_out = l0_w_out.astype(BF16)
    mix0 = (u0, attn, w_out[:CONV_WIDTH], w_out[CONV_WIDTH:])

    tm_ffn = _pick(s, 512)

    def ffn(xx, mix, g, w_up, cw, cb, w_down, final):
        taps = jnp.concatenate([cw.astype(F32), cb.astype(F32)[None],
                                jnp.zeros((SUBLANES - FFN_K - 1, 2 * D_FF), F32)], axis=0)
        return _ffn(xx, mix, row2(g), w_up.astype(BF16), taps, w_down.astype(BF16),
                    row2(final_norm), tm=tm_ffn, nsub=1, final_norm=final)

    x2 = ffn(x, mix0, l0_ffn_norm, l0_w_up, l0_ffn_conv_w, l0_ffn_conv_b, l0_w_down, False)

    rows = _pick(s // SSM_L, 128)
    nlev = int(math.log2(rows))
    mb, tt, mc, lp, dd = _ssm_tables(l1_log_dt, l1_a_re, l1_a_im, l1_b_re, l1_b_im,
                                     l1_c_re, l1_c_im, l1_d, nlev)
    x3 = _mix1(x2, row2(l1_mix_norm), l1_w_in.astype(BF16), mb, tt, mc, lp, dd,
               l1_w_glu.astype(BF16), row2(l1_b_glu), rows=rows)
    return ffn(x3, None, l1_ffn_norm, l1_w_up, l1_ffn_conv_w, l1_ffn_conv_b, l1_w_down, True)
```

```python
import functools
import math

import jax
import jax.numpy as jnp
from jax import lax
from jax.experimental import pallas as pl
from jax.experimental.pallas import tpu as pltpu

F32 = jnp.float32
BF16 = jnp.bfloat16

D_MODEL = 1024
EPS = 1e-6
LN_EPS = 1e-5
CONV_WIDTH = 512
CONV_K = 31
N_HEADS = 8
QK_NOPE_DIM = 64
QK_ROPE_DIM = 32
ROPE_HALF = QK_ROPE_DIM // 2
V_HEAD_DIM = 64
Q_LORA_RANK = 256
KV_LORA_RANK = 128
ROPE_BASE = 10000.0
ATTN_WIDTH = N_HEADS * V_HEAD_DIM
SSM_WIDTH = 512
SSM_GROUP = 16
SSM_GROUPS = SSM_WIDTH // SSM_GROUP
SSM_STATE = 64
D_FF = 2816
FFN_K = 3

LANES = 128
SUBLANES = 8
V_ROWS = V_HEAD_DIM + 16
HEAD_PAD = LANES
QK_WIDTH = N_HEADS * HEAD_PAD
CONV_HALO = 32
CONV_ROWS = 64
FFN_HALO = SUBLANES
FFN_CHUNK = 256
FFN_NCHUNK = D_FF // FFN_CHUNK
SSM_L = 8
SSM_PAIRS = SSM_GROUPS // 2
SSM_PAIR_CH = 2 * SSM_GROUP
SSM_TILE = SSM_L * SSM_PAIR_CH
SSM_ROW = SSM_L * SSM_WIDTH
VMEM_LIMIT = 56 * 1024 * 1024


def _cparams(sem):
    return pltpu.CompilerParams(dimension_semantics=sem, vmem_limit_bytes=VMEM_LIMIT)


def _rms(x, g):
    return x * lax.rsqrt(jnp.mean(x * x, axis=-1, keepdims=True) + EPS) * g


def _sigmoid(x):
    return 1.0 / (1.0 + jnp.exp(-x))


def _rope128(t, cos, sin_up, sin_dn):
    return (t * cos + pltpu.roll(t, ROPE_HALF, axis=1) * sin_up
            + pltpu.roll(t, LANES - ROPE_HALF, axis=1) * sin_dn)


def _rope_unpack(packed):
    lane = lax.broadcasted_iota(jnp.int32, packed.shape, 1)
    r1, r2 = QK_NOPE_DIM, QK_NOPE_DIM + ROPE_HALF
    cos = jnp.where(lane < r1, 1.0, packed)
    up = pltpu.roll(packed, r2, axis=1)
    dn = pltpu.roll(packed, r1 - ROPE_HALF, axis=1)
    sin_up = jnp.where(lane >= r2, jnp.where(lane < r2 + ROPE_HALF, up, 0.0), 0.0)
    sin_dn = jnp.where(lane >= r1, jnp.where(lane < r2, dn, 0.0), 0.0)
    return cos, sin_up, sin_dn


def _pre0_kernel(x_ref, g_ref, w_in_ref, rope_ref, cw_ref, cb_ref, lng_ref, lnb_ref,
                 qn_ref, kvn_ref, wuq_ref, wukv_ref, wvt_ref,
                 u_ref, q_ref, k_ref, vt_ref, ubuf, ushift, h_scr, q_scr, kv_scr, vt_scr, *, tm):
    sb = pl.program_id(1)

    @pl.when(sb == 0)
    def _():
        ubuf[0:CONV_HALO, :] = jnp.zeros((CONV_HALO, CONV_WIDTH), F32)

    hm = tm // 2
    n_in = w_in_ref.shape[1] // (2 * LANES)
    base = CONV_HALO - (CONV_K - 1)

    def in_proj(half):
        xn = _rms(x_ref[0, half * hm:(half + 1) * hm, :], g_ref[...]).astype(BF16)

        def piece(n):
            cols = slice(n * 2 * LANES, (n + 1) * 2 * LANES)
            h_scr[half, :, cols] = jnp.dot(xn, w_in_ref[:, cols], preferred_element_type=F32)
        return [functools.partial(piece, n) for n in range(n_in)]

    def qkv_proj(half):
        h = h_scr.at[half]
        o = 2 * CONV_WIDTH

        def q_piece():
            cq = _rms(h[:, o:o + Q_LORA_RANK], qn_ref[...]).astype(BF16)
            q_scr[...] = jnp.dot(cq, wuq_ref[...], preferred_element_type=F32)

        def kv_piece():
            o2 = o + Q_LORA_RANK
            ckv = _rms(h[:, o2:o2 + KV_LORA_RANK], kvn_ref[...]).astype(BF16)
            kv_scr[...] = jnp.dot(ckv, wukv_ref[...], preferred_element_type=F32)
            vt_scr[...] = lax.dot_general(wvt_ref[...], ckv, (((1,), (1,)), ((), ())), preferred_element_type=F32)
        return [q_piece, kv_piece]

    def conv(half, between):
        h = h_scr.at[half]
        t0 = half * hm
        ubuf[CONV_HALO + t0:CONV_HALO + t0 + hm, :] = h[:, :CONV_WIDTH] * _sigmoid(h[:, CONV_WIDTH:2 * CONV_WIDTH])
        span = hm + CONV_HALO - SUBLANES
        for r in range(1, SUBLANES):
            ushift[r - 1, t0:t0 + span, :] = ubuf[t0 + r:t0 + r + span, :]
        chunks = range(t0, t0 + hm, CONV_ROWS)
        for i, r0 in enumerate(chunks):
            acc = jnp.broadcast_to(cb_ref[...], (CONV_ROWS, CONV_WIDTH))
            for o in range(base, base + CONV_K):
                r, al = o % SUBLANES, o - o % SUBLANES + r0
                tap = ushift[r - 1, al:al + CONV_ROWS, :] if r else ubuf[al:al + CONV_ROWS, :]
                acc = acc + cw_ref[o - base:o - base + 1, :] * tap
            mu = jnp.mean(acc, axis=-1, keepdims=True)
            cen = acc - mu
            var = jnp.mean(cen * cen, axis=-1, keepdims=True)
            y = cen * lax.rsqrt(var + LN_EPS) * lng_ref[...] + lnb_ref[...]
            u_ref[0, r0:r0 + CONV_ROWS, :] = (y * _sigmoid(y)).astype(BF16)
            for thunk in between[i * len(between) // len(chunks):(i + 1) * len(between) // len(chunks)]:
                thunk()

    def rotary_out(half):
        rows = slice(half * hm, (half + 1) * hm)
        cos, sin_up, sin_dn = _rope_unpack(rope_ref[rows, :])
        o = 2 * CONV_WIDTH + Q_LORA_RANK + KV_LORA_RANK
        kr = _rope128(h_scr[half, :, o:o + LANES], cos, sin_up, sin_dn)
        sin = sin_up - sin_dn
        for hd in range(N_HEADS):
            sl = slice(hd * HEAD_PAD, (hd + 1) * HEAD_PAD)
            rot = slice(QK_WIDTH + hd * HEAD_PAD, QK_WIDTH + (hd + 1) * HEAD_PAD)
            q_ref[0, rows, sl] = (q_scr[:, sl] * cos + q_scr[:, rot] * sin).astype(BF16)
            k_ref[0, rows, sl] = (kv_scr[:, sl] + kr).astype(BF16)
        vt = vt_scr[...]
        ones = jnp.ones((V_ROWS - V_HEAD_DIM, hm), F32)
        parts = [blk for hd in range(N_HEADS) for blk in (vt[hd * V_HEAD_DIM:(hd + 1) * V_HEAD_DIM], ones)]
        vt_ref[0, 0, :, rows] = jnp.concatenate(parts, axis=0).astype(BF16)

    for thunk in in_proj(0):
        thunk()
    conv(0, in_proj(1))
    conv(1, qkv_proj(0))
    rotary_out(0)
    for thunk in qkv_proj(1):
        thunk()
    rotary_out(1)
    ubuf[0:CONV_HALO, :] = ubuf[tm:tm + CONV_HALO, :]


def _pre0(x, g, w_in, rope, cw, cb, lng, lnb, qn, kvn, wuq, wukv, wvt, *, tm):
    b, s, d = x.shape
    const = lambda shape: pl.BlockSpec(shape, lambda i, j: (0,) * len(shape))
    row = lambda w: pl.BlockSpec((1, tm, w), lambda i, j: (i, j, 0))
    return pl.pallas_call(
        functools.partial(_pre0_kernel, tm=tm),
        grid=(b, s // tm),
        in_specs=[row(d), const(g.shape), const(w_in.shape),
                  pl.BlockSpec((tm, LANES), lambda i, j: (j, 0)),
                  const(cw.shape), const(cb.shape), const(lng.shape), const(lnb.shape),
                  const(qn.shape), const(kvn.shape), const(wuq.shape), const(wukv.shape), const(wvt.shape)],
        out_specs=[row(CONV_WIDTH), row(QK_WIDTH), row(QK_WIDTH),
                   pl.BlockSpec((1, 1, N_HEADS * V_ROWS, tm), lambda i, j: (i, j, 0, 0))],
        out_shape=[jax.ShapeDtypeStruct((b, s, CONV_WIDTH), BF16),
                   jax.ShapeDtypeStruct((b, s, QK_WIDTH), BF16),
                   jax.ShapeDtypeStruct((b, s, QK_WIDTH), BF16),
                   jax.ShapeDtypeStruct((b, s // tm, N_HEADS * V_ROWS, tm), BF16)],
        scratch_shapes=[pltpu.VMEM((tm + CONV_HALO, CONV_WIDTH), F32),
                        pltpu.VMEM((SUBLANES - 1, tm + CONV_HALO, CONV_WIDTH), F32),
                        pltpu.VMEM((2, tm // 2, w_in.shape[1]), F32),
                        pltpu.VMEM((tm // 2, wuq.shape[1]), F32),
                        pltpu.VMEM((tm // 2, wukv.shape[1]), F32),
                        pltpu.VMEM((wvt.shape[0], tm // 2), F32)],
        compiler_params=_cparams(("arbitrary", "arbitrary")),
        name="pre0",
    )(x, g, w_in, rope, cw, cb, lng, lnb, qn, kvn, wuq, wukv, wvt)


NEG_BIG = -1e30
ATTN_HEADS = 4
KV_UNROLL = 4


def _attn_kernel(q_ref, k_ref, vt_ref, o_ref, *s_refs, t):
    qi = pl.program_id(2)
    nh = len(s_refs)

    def scores(j, hh):
        lanes = slice(hh * HEAD_PAD, (hh + 1) * HEAD_PAD)
        k = k_ref[0, pl.ds(pl.multiple_of(j * t, t), t), lanes]
        s_refs[hh][...] = lax.dot_general(k, q_ref[0, :, lanes], (((1,), (1,)), ((), ())),
                                          preferred_element_type=F32)

    def update(j, hh, state, masked):
        m, acc = state
        st = s_refs[hh][...]
        if masked:
            key = lax.broadcasted_iota(jnp.int32, (t, t), 0)
            qry = lax.broadcasted_iota(jnp.int32, (t, t), 1)
            st = jnp.where(key <= qry, st, NEG_BIG)
        m_new = jnp.maximum(m, jnp.max(st, axis=0, keepdims=True))
        alpha = jnp.exp2(m - m_new)
        p = jnp.exp2(st - m_new).astype(BF16)
        vt = vt_ref[0, j, hh * V_ROWS:(hh + 1) * V_ROWS, :]
        acc = alpha * acc + jnp.dot(vt, p, preferred_element_type=F32)
        return m_new, acc

    def block(j, carry, masked):
        out = ()
        for hh in range(nh):
            if hh + 1 < nh:
                scores(j, hh + 1)
            elif not masked:
                scores(j + 1, 0)
            out += update(j, hh, carry[2 * hh:2 * hh + 2], masked)
        return out

    body = functools.partial(block, masked=False)

    def unrolled(i, c):
        for u in range(KV_UNROLL):
            c = body(KV_UNROLL * i + u, c)
        return c

    init = (jnp.full((1, t), NEG_BIG, F32), jnp.zeros((V_ROWS, t), F32)) * nh
    scores(0, 0)
    carry = lax.fori_loop(0, qi // KV_UNROLL, unrolled, init)
    carry = lax.fori_loop(qi // KV_UNROLL * KV_UNROLL, qi, body, carry)
    carry = block(qi, carry, True)
    d = V_HEAD_DIM
    ot = jnp.concatenate([carry[2 * hh + 1][:d] / carry[2 * hh + 1][d:d + 1] for hh in range(nh)], axis=0)
    o_ref[0] = ot.T.astype(o_ref.dtype)


def _attention(q, k, vt, *, t):
    b, s, _ = q.shape
    assert vt.shape == (b, s // t, N_HEADS * V_ROWS, t)
    nh = ATTN_HEADS
    return pl.pallas_call(
        functools.partial(_attn_kernel, t=t),
        grid=(b, N_HEADS // nh, s // t),
        in_specs=[pl.BlockSpec((1, t, nh * HEAD_PAD), lambda i, h, j: (i, j, h)),
                  pl.BlockSpec((1, s, nh * HEAD_PAD), lambda i, h, j: (i, 0, h)),
                  pl.BlockSpec((1, s // t, nh * V_ROWS, t), lambda i, h, j: (i, 0, h, 0))],
        out_specs=pl.BlockSpec((1, t, nh * V_HEAD_DIM), lambda i, h, j: (i, j, h)),
        out_shape=jax.ShapeDtypeStruct((b, s, ATTN_WIDTH), BF16),
        scratch_shapes=[pltpu.VMEM((t, t), F32)] * nh,
        compiler_params=_cparams(("arbitrary", "arbitrary", "arbitrary")),
        name="attn",
    )(q, k, vt)


def _ffn_kernel(*refs, tm, mix_out, final_norm):
    if mix_out:
        x_ref, u_ref, a_ref, wu_ref, wa_ref, *refs = refs
    else:
        x_ref, *refs = refs
    g_ref, wup_ref, cw_ref, wd_ref, fg_ref, o_ref, xn_scr, hbuf0, hbuf1, abuf0, abuf1, carry = refs
    sb = pl.program_id(1)
    nsub = x_ref.shape[1] // tm
    halves = lambda c: ((slice(0, FFN_CHUNK), c * FFN_CHUNK),
                        (slice(FFN_CHUNK, 2 * FFN_CHUNK), D_FF + c * FFN_CHUNK))

    def start(r):
        rows = slice(r * tm, (r + 1) * tm)
        x = x_ref[0, rows, :]
        if mix_out:
            x = (x + jnp.dot(u_ref[0, rows, :], wu_ref[...], preferred_element_type=F32)
                 + jnp.dot(a_ref[0, rows, :], wa_ref[...], preferred_element_type=F32))
        xn_scr[r] = _rms(x, g_ref[...]).astype(BF16)
        o_ref[0, rows, :] = x

    def up(r, c, hbuf):
        first = sb == 0 if r == 0 else False
        for lanes, col in halves(c):
            h = jnp.dot(xn_scr[r], wup_ref[:, col:col + FFN_CHUNK], preferred_element_type=F32)
            hbuf[0:FFN_HALO, lanes] = jnp.where(first, 0.0, carry[c, :, lanes])
            hbuf[FFN_HALO:FFN_HALO + tm, lanes] = h
            carry[c, :, lanes] = h[tm - FFN_HALO:, :]

    def gate(c, hbuf, abuf):
        cv = []
        for lanes, col in halves(c):
            w = cw_ref[:, col:col + FFN_CHUNK]
            v = w[FFN_K - 1:FFN_K, :] * hbuf[FFN_HALO:FFN_HALO + tm, lanes] + w[FFN_K:FFN_K + 1, :]
            for k in range(FFN_K - 1):
                off = FFN_HALO - (FFN_K - 1) + k
                v = v + w[k:k + 1, :] * hbuf[off:off + tm, lanes]
            cv.append(v)
        abuf[...] = (cv[0] * _sigmoid(cv[0]) * cv[1]).astype(BF16)

    def down(r, c, abuf):
        rows = slice(r * tm, (r + 1) * tm)
        o_ref[0, rows, :] += jnp.dot(abuf[...], wd_ref[c * FFN_CHUNK:(c + 1) * FFN_CHUNK, :],
                                     preferred_element_type=F32)
        if final_norm and c == FFN_NCHUNK - 1:
            o_ref[0, rows, :] = _rms(o_ref[0, rows, :], fg_ref[...])

    hb, ab = (hbuf0, hbuf1), (abuf0, abuf1)
    items = [(r, c) for r in range(nsub) for c in range(FFN_NCHUNK)]
    start(0)
    up(0, 0, hb[0])
    for i, (r, c) in enumerate(items):
        if i + 1 < len(items):
            rn, cn = items[i + 1]
            if cn == 0:
                start(rn)
            up(rn, cn, hb[(i + 1) % 2])
        if i:
            down(*items[i - 1], ab[(i - 1) % 2])
        gate(c, hb[i % 2], ab[i % 2])
    down(*items[-1], ab[(len(items) - 1) % 2])


def _ffn(x, mix, g, wup, cw, wd, fg, *, tm, nsub, final_norm):
    b, s, d = x.shape
    const = lambda a: pl.BlockSpec(a.shape, lambda i, j: (0,) * a.ndim, pipeline_mode=pl.Buffered(1))
    row = lambda w: pl.BlockSpec((1, nsub * tm, w), lambda i, j: (i, j, 0))
    ins, specs = [x], [row(d)]
    if mix is not None:
        u, a, wu, wa = mix
        ins += [u, a, wu, wa]
        specs += [row(u.shape[-1]), row(a.shape[-1]), const(wu), const(wa)]
    ins += [g, wup, cw, wd, fg]
    specs += [const(g), const(wup), const(cw), const(wd), const(fg)]
    return pl.pallas_call(
        functools.partial(_ffn_kernel, tm=tm, mix_out=mix is not None, final_norm=final_norm),
        grid=(b, s // (nsub * tm)),
        in_specs=specs,
        out_specs=row(d),
        out_shape=jax.ShapeDtypeStruct((b, s, d), F32),
        scratch_shapes=[pltpu.VMEM((nsub, tm, d), BF16),
                        pltpu.VMEM((tm + FFN_HALO, 2 * FFN_CHUNK), F32),
                        pltpu.VMEM((tm + FFN_HALO, 2 * FFN_CHUNK), F32),
                        pltpu.VMEM((tm, FFN_CHUNK), BF16),
                        pltpu.VMEM((tm, FFN_CHUNK), BF16),
                        pltpu.VMEM((FFN_NCHUNK, FFN_HALO, 2 * FFN_CHUNK), F32)],
        compiler_params=_cparams(("arbitrary", "arbitrary")),
        name="ffn_final" if final_norm else "ffn",
    )(*ins)


def _regroup(slabs, n_out, pieces):
    rows = slabs[0].shape[0]
    quarter = lax.broadcasted_iota(jnp.int32, (rows, LANES), 1) // SSM_PAIR_CH
    out = []
    for d in range(n_out):
        acc = None
        for dst_q, (src, src_q) in enumerate(pieces(d)):
            shift = ((dst_q - src_q) * SSM_PAIR_CH) % LANES
            piece = pltpu.roll(slabs[src], shift, axis=1) if shift else slabs[src]
            acc = piece if acc is None else jnp.where(quarter == dst_q, piece, acc)
        out.append(acc)
    return out


def _shift_rows(x, n, row):
    return jnp.where(row >= n, pltpu.roll(x, n, axis=0), 0.0)


def _mix1_kernel(x_ref, g_ref, win_ref, mb_ref, t_ref, mc_ref, lp_ref, d_ref, wglu_ref, bglu_ref,
                 o_ref, carry, u_slab, y_slab, *, rows, nlev):
    sb = pl.program_id(1)

    @pl.when(sb == 0)
    def _():
        carry[...] = jnp.zeros(carry.shape, F32)

    x = x_ref[0]
    xn = _rms(x, g_ref[...]).astype(BF16)
    u = jnp.dot(xn, win_ref[...], preferred_element_type=F32)
    nch = SSM_WIDTH // LANES
    for s in range(nch):
        u_slab[s] = u[:, s * LANES:(s + 1) * LANES]
    nslab = SSM_ROW // LANES
    u_slabs = [u_slab[i % nch, pl.ds(i // nch, rows, stride=SSM_L), :] for i in range(nslab)]
    a_slabs = _regroup(u_slabs, nslab,
                       lambda d: [(4 * (4 * (d % 2) + i) + (d // 2) // 4, (d // 2) % 4) for i in range(4)])
    row = lax.broadcasted_iota(jnp.int32, (rows, LANES), 0)
    first = row == 0
    y_slabs = []
    for p in range(SSM_PAIRS):
        a = jnp.concatenate(a_slabs[2 * p:2 * p + 2], axis=1)
        a16 = a.astype(BF16)
        lo = slice(p * SSM_TILE, p * SSM_TILE + LANES)
        hi = slice(p * SSM_TILE + LANES, (p + 1) * SSM_TILE)
        sloc = jnp.dot(a16, mb_ref[p], preferred_element_type=F32)
        sr, si = sloc[:, :LANES], sloc[:, LANES:]
        cr, ci = carry[0:1, lo], carry[0:1, hi]
        lr, li = lp_ref[0:1, lo], lp_ref[0:1, hi]
        sr = sr + jnp.where(first, lr * cr - li * ci, 0.0)
        si = si + jnp.where(first, lr * ci + li * cr, 0.0)
        for lev in range(nlev):
            n = 1 << lev
            lr, li = lp_ref[lev:lev + 1, lo], lp_ref[lev:lev + 1, hi]
            pr, pi = _shift_rows(sr, n, row), _shift_rows(si, n, row)
            sr, si = sr + (lr * pr - li * pi), si + (lr * pi + li * pr)
        xr = jnp.where(first, cr, pltpu.roll(sr, 1, axis=0))
        xi = jnp.where(first, ci, pltpu.roll(si, 1, axis=0))
        carry[0:1, lo] = sr[rows - 1:rows, :]
        carry[0:1, hi] = si[rows - 1:rows, :]
        xp = jnp.concatenate([xr, xi], axis=1).astype(BF16)
        y = (jnp.dot(a16, t_ref[p], preferred_element_type=F32)
             + jnp.dot(xp, mc_ref[p], preferred_element_type=F32)
             + d_ref[:, p * SSM_TILE:(p + 1) * SSM_TILE] * a)
        y = jax.nn.gelu(y, approximate=True)
        y_slabs += [y[:, :LANES], y[:, LANES:]]
    o_slabs = _regroup(y_slabs, nslab,
                       lambda s: [(2 * (4 * (s % 4) + j) + (s // 4) // 4, (s // 4) % 4) for j in range(4)])
    for i in range(nslab):
        y_slab[i % nch, pl.ds(i // nch, rows, stride=SSM_L), :] = o_slabs[i]
    y = jnp.concatenate([y_slab[s] for s in range(nch)], axis=1).astype(BF16)
    z = jnp.dot(y, wglu_ref[...], preferred_element_type=F32) + bglu_ref[...]
    o_ref[0] = x + z[:, :D_MODEL] * _sigmoid(z[:, D_MODEL:])


def _mix1(x, g, win, mb, tt, mc, lp, dd, wglu, bglu, *, rows):
    b, s, d = x.shape
    tok = rows * SSM_L
    nlev = int(math.log2(rows))
    assert 1 << nlev == rows and lp.shape[0] >= nlev
    const = lambda a: pl.BlockSpec(a.shape, lambda i, j: (0,) * a.ndim, pipeline_mode=pl.Buffered(1))
    return pl.pallas_call(
        functools.partial(_mix1_kernel, rows=rows, nlev=nlev),
        grid=(b, s // tok),
        in_specs=[pl.BlockSpec((1, tok, d), lambda i, j: (i, j, 0)),
                  const(g), const(win), const(mb), const(tt), const(mc), const(lp), const(dd),
                  const(wglu), const(bglu)],
        out_specs=pl.BlockSpec((1, tok, d), lambda i, j: (i, j, 0)),
        out_shape=jax.ShapeDtypeStruct((b, s, d), F32),
        scratch_shapes=[pltpu.VMEM((SUBLANES, SSM_ROW), F32),
                        pltpu.VMEM((SSM_WIDTH // LANES, tok, LANES), F32),
                        pltpu.VMEM((SSM_WIDTH // LANES, tok, LANES), F32)],
        compiler_params=_cparams(("arbitrary", "arbitrary")),
        name="mix1",
    )(x, g, win, mb, tt, mc, lp, dd, wglu, bglu)


def _ssm_tables(log_dt, a_re, a_im, b_re, b_im, c_re, c_im, d_skip, nlev):
    g, p, c, L = SSM_GROUPS, SSM_STATE, SSM_GROUP, SSM_L
    dt = jnp.exp(log_dt.astype(F32))[:, None]
    ar, ai = a_re.astype(F32), a_im.astype(F32)
    mag = jnp.exp(ar * dt)
    lb_re, lb_im = mag * jnp.cos(ai * dt), mag * jnp.sin(ai * dt)
    den = ar * ar + ai * ai
    nr, ni = lb_re - 1.0, lb_im
    f_re = (nr * ar + ni * ai) / den
    f_im = (ni * ar - nr * ai) / den
    br, bi = b_re.astype(F32), b_im.astype(F32)
    bb_re = f_re[..., None] * br - f_im[..., None] * bi
    bb_im = f_re[..., None] * bi + f_im[..., None] * br

    def lam_pow(n):
        n = jnp.asarray(n, F32)[:, None, None]
        m = jnp.exp(n * (ar * dt))
        return m * jnp.cos(n * (ai * dt)), m * jnp.sin(n * (ai * dt))

    cr, ci = c_re.astype(F32), c_im.astype(F32)
    pr, pi = lam_pow(jnp.arange(L + 1))
    lbr = pr[..., None] * bb_re[None] - pi[..., None] * bb_im[None]
    lbi = pr[..., None] * bb_im[None] + pi[..., None] * bb_re[None]
    kk = (jnp.einsum('gop,ngpc->ngoc', cr, lbr[:L]) - jnp.einsum('gop,ngpc->ngoc', ci, lbi[:L]))
    cat = jnp.concatenate
    q = SSM_PAIRS

    def pair_diag(a0, a1):
        z = jnp.zeros_like(a0)
        return cat([cat([a0, z], axis=2), cat([z, a1], axis=2)], axis=1)

    kt = kk.transpose(0, 1, 3, 2).reshape(L, q, 2, c, c)
    kbd = [pair_diag(kt[n, :, 0], kt[n, :, 1]) for n in range(L)]
    zero = jnp.zeros_like(kbd[0])
    tt = cat([cat([kbd[t2 - t1] if t2 >= t1 else zero for t2 in range(L)], axis=2) for t1 in range(L)], axis=1)
    def mb_rows(n):
        r = lbr[n].transpose(0, 2, 1).reshape(q, 2, c, p)
        i = lbi[n].transpose(0, 2, 1).reshape(q, 2, c, p)
        return cat([pair_diag(r[:, 0], r[:, 1]), pair_diag(i[:, 0], i[:, 1])], axis=2)
    mb = cat([mb_rows(L - 1 - t) for t in range(L)], axis=1)
    p1r, p1i = pr[1:], pi[1:]
    clr = cr[None] * p1r[:, :, None, :] - ci[None] * p1i[:, :, None, :]
    cli = cr[None] * p1i[:, :, None, :] + ci[None] * p1r[:, :, None, :]
    def mc_cols(t):
        r = clr[t].transpose(0, 2, 1).reshape(q, 2, p, c)
        i = -cli[t].transpose(0, 2, 1).reshape(q, 2, p, c)
        return cat([pair_diag(r[:, 0], r[:, 1]), pair_diag(i[:, 0], i[:, 1])], axis=1)
    mc = cat([mc_cols(t) for t in range(L)], axis=2)
    sr, si = lam_pow(L * (2 ** jnp.arange(nlev)))
    lp = jnp.stack([sr.reshape(nlev, SSM_PAIRS, 2 * p), si.reshape(nlev, SSM_PAIRS, 2 * p)], axis=2)
    lp = lp.reshape(nlev, SSM_ROW)
    lp = jnp.pad(lp, ((0, (-nlev) % SUBLANES), (0, 0)))
    dd = jnp.broadcast_to(d_skip.astype(F32).reshape(SSM_PAIRS, 1, SSM_PAIR_CH),
                          (SSM_PAIRS, L, SSM_PAIR_CH)).reshape(1, SSM_ROW)
    return mb.astype(BF16), tt.astype(BF16), mc.astype(BF16), lp, dd


def _head_pad_cols(nope, rope):
    k = nope.shape[0]
    parts = [nope]
    if rope is not None:
        parts.append(rope)
    used = sum(t.shape[-1] for t in parts)
    parts.append(jnp.zeros((k, N_HEADS, HEAD_PAD - used), nope.dtype))
    return jnp.concatenate(parts, axis=-1).reshape(k, QK_WIDTH)


def _rope_tables(s):
    inv = ROPE_BASE ** (-jnp.arange(ROPE_HALF, dtype=F32) / ROPE_HALF)
    per_row = LANES // ROPE_HALF
    pos = (per_row * jnp.arange(s // per_row, dtype=F32)[:, None]
           + jnp.repeat(jnp.arange(per_row, dtype=F32), ROPE_HALF)[None, :])
    ang = pos * jnp.tile(inv, per_row)[None, :]
    cos, sin = jnp.cos(ang).reshape(s, ROPE_HALF), jnp.sin(ang).reshape(s, ROPE_HALF)
    gap = jnp.zeros((s, QK_NOPE_DIM - QK_ROPE_DIM), F32)
    tail = jnp.ones((s, HEAD_PAD - QK_NOPE_DIM - QK_ROPE_DIM), F32)
    return jnp.concatenate([sin, -sin, gap, cos, cos, tail], axis=1)


def _pick(total, want):
    t = min(total, want)
    assert total % t == 0
    return t


def kernel(x, l0_mix_norm, l0_w_in, l0_conv_w, l0_conv_b, l0_conv_ln_g, l0_conv_ln_b, l0_q_norm, l0_kv_norm, l0_w_uq, l0_w_ukv, l0_w_out, l0_ffn_norm, l0_w_up, l0_ffn_conv_w, l0_ffn_conv_b, l0_w_down, l1_mix_norm, l1_w_in, l1_log_dt, l1_a_re, l1_a_im, l1_b_re, l1_b_im, l1_c_re, l1_c_im, l1_d, l1_w_glu, l1_b_glu, l1_ffn_norm, l1_w_up, l1_ffn_conv_w, l1_ffn_conv_b, l1_w_down, final_norm):
    b, s, d = x.shape
    row2 = lambda v: v.reshape(1, -1).astype(F32)

    o = 2 * CONV_WIDTH + Q_LORA_RANK + KV_LORA_RANK
    w_kr = l0_w_in[:, o:]
    kr_blk = jnp.concatenate([jnp.zeros((d, QK_NOPE_DIM), F32), w_kr,
                              jnp.zeros((d, HEAD_PAD - QK_NOPE_DIM - QK_ROPE_DIM), F32)], axis=1)
    w_in0 = jnp.concatenate([l0_w_in[:, :o], kr_blk], axis=1).astype(BF16)
    scale = (QK_NOPE_DIM + QK_ROPE_DIM) ** -0.5 * math.log2(math.e)
    wq = (l0_w_uq * scale).reshape(Q_LORA_RANK, N_HEADS, QK_NOPE_DIM + QK_ROPE_DIM)
    wq_rot = jnp.concatenate([-wq[..., QK_NOPE_DIM + ROPE_HALF:], wq[..., QK_NOPE_DIM:QK_NOPE_DIM + ROPE_HALF]], axis=-1)
    wuq = jnp.concatenate([_head_pad_cols(wq[..., :QK_NOPE_DIM], wq[..., QK_NOPE_DIM:]),
                           _head_pad_cols(jnp.zeros_like(wq[..., :QK_NOPE_DIM]), wq_rot)], axis=1).astype(BF16)
    wkv = l0_w_ukv.reshape(KV_LORA_RANK, N_HEADS, QK_NOPE_DIM + V_HEAD_DIM)
    wukv = _head_pad_cols(wkv[..., :QK_NOPE_DIM], None).astype(BF16)
    wvt = wkv[..., QK_NOPE_DIM:].reshape(KV_LORA_RANK, ATTN_WIDTH).T.astype(BF16)
    cw0 = jnp.pad(l0_conv_w.astype(F32), ((0, CONV_HALO - CONV_K), (0, 0)))
    rope = _rope_tables(s)

    t_attn = _pick(s, 512)
    u0, q, k, vt = _pre0(x, row2(l0_mix_norm), w_in0, rope, cw0, row2(l0_conv_b),
                         row2(l0_conv_ln_g), row2(l0_conv_ln_b), row2(l0_q_norm), row2(l0_kv_norm),
                         wuq, wukv, wvt, tm=t_attn)
    attn = _attention(q, k, vt, t=t_attn)
    w_out = l0_w_out.astype(BF16)
    mix0 = (u0, attn, w_out[:CONV_WIDTH], w_out[CONV_WIDTH:])

    tm_ffn = _pick(s, 512)

    def ffn(xx, mix, g, w_up, cw, cb, w_down, final):
        taps = jnp.concatenate([cw.astype(F32), cb.astype(F32)[None],
                                jnp.zeros((SUBLANES - FFN_K - 1, 2 * D_FF), F32)], axis=0)
        return _ffn(xx, mix, row2(g), w_up.astype(BF16), taps, w_down.astype(BF16),
                    row2(final_norm), tm=tm_ffn, nsub=1, final_norm=final)

    x2 = ffn(x, mix0, l0_ffn_norm, l0_w_up, l0_ffn_conv_w, l0_ffn_conv_b, l0_w_down, False)

    rows = _pick(s // SSM_L, 128)
    nlev = int(math.log2(rows))
    mb, tt, mc, lp, dd = _ssm_tables(l1_log_dt, l1_a_re, l1_a_im, l1_b_re, l1_b_im,
                                     l1_c_re, l1_c_im, l1_d, nlev)
    x3 = _mix1(x2, row2(l1_mix_norm), l1_w_in.astype(BF16), mb, tt, mc, lp, dd,
               l1_w_glu.astype(BF16), row2(l1_b_glu), rows=rows)
    return ffn(x3, None, l1_ffn_norm, l1_w_up, l1_ffn_conv_w, l1_ffn_conv_b, l1_w_down, True)
```
